```python
import math
import jax, jax.numpy as jnp
from jax import lax
import numpy as np

D_MODEL = 1024
BATCH = 8
SEQ = 2048
DEPTH = 2
DEC_BATCH = 32
DEC_SEQ = 4
PAST_LEN = 8192
PAGE_SIZE = 128

N_HEADS = 16
N_KV_HEADS = 4
HEAD_DIM = D_MODEL // N_HEADS
Q_PER_KV = N_HEADS // N_KV_HEADS
IDX_HEADS = 16
IDX_DIM = 64
TOPK_MAX = 256
Q_BLOCK = 128
ATTN_SCALE = HEAD_DIM ** -0.5
INDEX_SCALE = (IDX_HEADS ** -0.5) * (IDX_DIM ** -0.5)
Q_COLS = N_HEADS * HEAD_DIM
KV_COLS = N_KV_HEADS * HEAD_DIM
QI_COLS = IDX_HEADS * IDX_DIM
IN_SPLITS = (Q_COLS, KV_COLS, KV_COLS, QI_COLS, IDX_DIM, IDX_HEADS)
IN_COLS = Q_COLS + 2 * KV_COLS + QI_COLS + IDX_DIM + IDX_HEADS
SSM_GROUP = 16
SSM_GROUPS = D_MODEL // SSM_GROUP
SSM_STATE = 64
DT_MIN = 0.001
DT_MAX = 0.1
D_FF = 2816
N_EXPERTS = 8
TOP_K_EXPERTS = 2
D_FF_EXPERT = 2816
N_ATTN_LAYERS = (DEPTH + 1) // 2
N_SSM_LAYERS = DEPTH // 2
RMS_EPS = 1e-6

kernel_name = "dsa_s5_hybrid_decode_step"


def rms_norm(x, g):
    xf = x.astype(jnp.float32)
    y = xf * lax.rsqrt(jnp.mean(xf * xf, axis=-1, keepdims=True) + RMS_EPS) * g.astype(jnp.float32)
    return y.astype(x.dtype)


def attn_project(x, norm_g, w_in, g_q, g_k, g_kidx):
    b, t, _ = x.shape
    xn = rms_norm(x, norm_g)
    offs = np.cumsum(IN_SPLITS)[:-1].tolist()
    q, k, v, qi, ki, wi = jnp.split(xn @ w_in, offs, axis=-1)
    q = rms_norm(q.reshape(b, t, N_HEADS, HEAD_DIM), g_q)
    k = rms_norm(k.reshape(b, t, N_KV_HEADS, HEAD_DIM), g_k)
    v = v.reshape(b, t, N_KV_HEADS, HEAD_DIM)
    qi = qi.reshape(b, t, IDX_HEADS, IDX_DIM)
    ki = rms_norm(ki, g_kidx)
    return q, k, v, qi, ki, wi


def indexer_topk(qi, wi, ki, q_pos, ksel):
    s = jnp.einsum('bthd,bsd->bths', qi.astype(jnp.float32), ki.astype(jnp.float32))
    score = jnp.einsum('bths,bth->bts', jax.nn.relu(s), wi.astype(jnp.float32)) * INDEX_SCALE
    key_pos = jnp.arange(ki.shape[1])
    score = jnp.where(key_pos[None, None, :] <= q_pos[None, :, None], score, -jnp.inf)
    _, idx = lax.top_k(score, ksel)
    return idx


def sparse_attend(q, kg, vg, idx, q_pos):
    b, t = q.shape[:2]
    qg = q.reshape(b, t, N_KV_HEADS, Q_PER_KV, HEAD_DIM).astype(jnp.float32)
    logits = jnp.einsum('btngd,btsnd->btngs', qg, kg.astype(jnp.float32)) * ATTN_SCALE
    valid = (idx <= q_pos[None, :, None])[:, :, None, None, :]
    p = jax.nn.softmax(jnp.where(valid, logits, -jnp.inf), axis=-1)
    out = jnp.einsum('btngs,btsnd->btngd', p, vg.astype(jnp.float32))
    return out.reshape(b, t, N_HEADS * HEAD_DIM).astype(q.dtype)


def attn_prompt(x, norm_g, w_in, g_q, g_k, g_kidx, w_o):
    q, k, v, qi, ki, wi = attn_project(x, norm_g, w_in, g_q, g_k, g_kidx)
    b, t = x.shape[:2]
    nb = t // Q_BLOCK
    ksel = min(TOPK_MAX, t // 4)
    bidx = jnp.arange(b)[:, None, None]

    def to_blocks(a):
        return jnp.moveaxis(a.reshape((b, nb, Q_BLOCK) + a.shape[2:]), 1, 0)

    def block_fn(args):
        blk, qb, qib, wb = args
        q_pos = blk * Q_BLOCK + jnp.arange(Q_BLOCK)
        idx = indexer_topk(qib, wb, ki, q_pos, ksel)
        return sparse_attend(qb, k[bidx, idx], v[bidx, idx], idx, q_pos)

    o = lax.map(block_fn, (jnp.arange(nb), to_blocks(q), to_blocks(qi), to_blocks(wi)))
    o = jnp.moveaxis(o, 0, 1).reshape(b, t, N_HEADS * HEAD_DIM)
    return o @ w_o, k, v, ki


def attn_sample(x, cache_k_l, cache_v_l, cache_ki_l, page_table, norm_g, w_in, g_q, g_k, g_kidx, w_o):
    q, k, v, qi, ki, wi = attn_project(x, norm_g, w_in, g_q, g_k, g_kidx)
    db, t = x.shape[:2]
    past = page_table.shape[1] * PAGE_SIZE
    ki_past = cache_ki_l[page_table].reshape(db, past, IDX_DIM).astype(ki.dtype)
    ki_all = jnp.concatenate([ki_past, ki], axis=1)
    q_pos = past + jnp.arange(t)
    ksel = min(TOPK_MAX, (past + t) // 4)
    idx = indexer_topk(qi, wi, ki_all, q_pos, ksel)
    bidx = jnp.arange(db)[:, None, None]
    past_idx = jnp.clip(idx, 0, past - 1)
    phys = page_table[bidx, past_idx // PAGE_SIZE]
    off = past_idx % PAGE_SIZE
    new_idx = jnp.clip(idx - past, 0, t - 1)
    is_new = (idx >= past)[..., None, None]
    kg = jnp.where(is_new, k[bidx, new_idx], cache_k_l[phys, off].astype(k.dtype))
    vg = jnp.where(is_new, v[bidx, new_idx], cache_v_l[phys, off].astype(v.dtype))
    o = sparse_attend(q, kg, vg, idx, q_pos)
    return o @ w_o, k, v, ki


def _complex_affine_combine(e1, e2):
    a1r, a1i, b1r, b1i = e1
    a2r, a2i, b2r, b2i = e2
    ar = a1r * a2r - a1i * a2i
    ai = a1r * a2i + a1i * a2r
    br = a2r * b1r - a2i * b1i + b2r
    bi = a2r * b1i + a2i * b1r + b2i
    return (ar, ai, br, bi)


def s5_layer(x, s0_re, s0_im, norm_g, w_in, a_re, a_im, log_dt, b_re, b_im, c_re, c_im, d_skip, w_glu):
    b, t, _ = x.shape
    f32 = jnp.float32
    xn = rms_norm(x, norm_g)
    u = (xn @ w_in).astype(f32)
    ug = u.reshape(b, t, SSM_GROUPS, SSM_GROUP)
    dt = jnp.exp(log_dt.astype(f32))[:, None]
    lr = a_re.astype(f32) * dt
    li = a_im.astype(f32) * dt
    mag = jnp.exp(lr)
    abr = mag * jnp.cos(li)
    abi = mag * jnp.sin(li)
    den = lr * lr + li * li
    fr = ((abr - 1.0) * lr + abi * li) / den
    fi = (abi * lr - (abr - 1.0) * li) / den
    fr = fr * dt
    fi = fi * dt
    bur = jnp.einsum('btgc,gpc->btgp', ug, b_re.astype(f32))
    bui = jnp.einsum('btgc,gpc->btgp', ug, b_im.astype(f32))
    xr = fr * bur - fi * bui
    xi = fr * bui + fi * bur
    ar = jnp.broadcast_to(abr, xr.shape)
    ai = jnp.broadcast_to(abi, xr.shape)
    acr, aci, hr, hi = lax.associative_scan(_complex_affine_combine, (ar, ai, xr, xi), axis=1)
    if s0_re is not None:
        s0r = s0_re.astype(f32)[:, None]
        s0i = s0_im.astype(f32)[:, None]
        hr = hr + acr * s0r - aci * s0i
        hi = hi + acr * s0i + aci * s0r
    y = jnp.einsum('btgp,gcp->btgc', hr, c_re.astype(f32)) - jnp.einsum('btgp,gcp->btgc', hi, c_im.astype(f32))
    y = y.reshape(b, t, D_MODEL) + d_skip.astype(f32) * u
    z = jax.nn.gelu(y).astype(x.dtype)
    ga, gb = jnp.split(z @ w_glu, 2, axis=-1)
    out = ga * jax.nn.sigmoid(gb)
    return out.astype(x.dtype), hr[:, -1], hi[:, -1]


def swiglu_ffn(x, norm_g, w_gate, w_up, w_down):
    xn = rms_norm(x, norm_g)
    return (jax.nn.silu(xn @ w_gate) * (xn @ w_up)) @ w_down


def moe_ffn(x, norm_g, w_router, b_router, w_gate, w_up, w_down):
    b, t, d = x.shape
    xn = rms_norm(x, norm_g).reshape(b * t, d)
    logits = xn.astype(jnp.float32) @ w_router.astype(jnp.float32) + b_router.astype(jnp.float32)
    top_v, top_i = lax.top_k(logits, TOP_K_EXPERTS)
    gates = jax.nn.softmax(top_v, axis=-1)
    combine = jnp.sum(jax.nn.one_hot(top_i, N_EXPERTS, dtype=jnp.float32) * gates[..., None], axis=1)
    out = jnp.zeros((b * t, d), jnp.float32)
    for e in range(N_EXPERTS):
        h = jax.nn.silu(xn @ w_gate[e]) * (xn @ w_up[e])
        out = out + combine[:, e:e + 1] * (h @ w_down[e]).astype(jnp.float32)
    return out.astype(x.dtype).reshape(b, t, d)


def setup_inputs(seed: int = 0) -> dict:
    key = jax.random.key(seed)
    ks = iter(jax.random.split(key, 64))

    def nrm(shape, scale):
        return jax.random.normal(next(ks), shape, jnp.float32) * scale

    def gain(shape):
        return 1.0 + nrm(shape, 0.02)

    n_pages = PAST_LEN // PAGE_SIZE
    n_pool = (DEC_BATCH * n_pages * 5) // 4
    na, ns = N_ATTN_LAYERS, N_SSM_LAYERS
    d = D_MODEL
    x_prompt = nrm((BATCH, SEQ, d), 1.0)
    x_sample = nrm((DEC_BATCH, DEC_SEQ, d), 1.0)
    cache_k = nrm((na, n_pool, PAGE_SIZE, N_KV_HEADS, HEAD_DIM), 1.0)
    cache_v = nrm((na, n_pool, PAGE_SIZE, N_KV_HEADS, HEAD_DIM), 1.0)
    cache_kidx = nrm((na, n_pool, PAGE_SIZE, IDX_DIM), 1.0)
    state_s5_re = nrm((ns, DEC_BATCH, SSM_GROUPS, SSM_STATE), 0.1)
    state_s5_im = nrm((ns, DEC_BATCH, SSM_GROUPS, SSM_STATE), 0.1)
    page_table = jax.random.permutation(next(ks), n_pool)[: DEC_BATCH * n_pages].reshape(DEC_BATCH, n_pages).astype(jnp.int32)
    attn_norm_g = gain((na, d))
    attn_w_in = nrm((na, d, IN_COLS), d ** -0.5)
    attn_g_q = gain((na, HEAD_DIM))
    attn_g_k = gain((na, HEAD_DIM))
    attn_g_kidx = gain((na, IDX_DIM))
    attn_w_o = nrm((na, N_HEADS * HEAD_DIM, d), (N_HEADS * HEAD_DIM) ** -0.5)
    dense_norm_g = gain((na, d))
    dense_w_gate = nrm((na, d, D_FF), d ** -0.5)
    dense_w_up = nrm((na, d, D_FF), d ** -0.5)
    dense_w_down = nrm((na, D_FF, d), D_FF ** -0.5)
    s5_norm_g = gain((ns, d))
    s5_w_in = nrm((ns, d, d), d ** -0.5)
    s5_a_re = -0.5 + nrm((ns, SSM_GROUPS, SSM_STATE), 0.01)
    s5_a_im = math.pi * jnp.arange(SSM_STATE, dtype=jnp.float32) + nrm((ns, SSM_GROUPS, SSM_STATE), 0.01)
    s5_log_dt = jax.random.uniform(next(ks), (ns, SSM_GROUPS), jnp.float32, math.log(DT_MIN), math.log(DT_MAX))
    s5_b_re = nrm((ns, SSM_GROUPS, SSM_STATE, SSM_GROUP), (2 * SSM_GROUP) ** -0.5)
    s5_b_im = nrm((ns, SSM_GROUPS, SSM_STATE, SSM_GROUP), (2 * SSM_GROUP) ** -0.5)
    s5_c_re = nrm((ns, SSM_GROUPS, SSM_GROUP, SSM_STATE), (2 * SSM_STATE) ** -0.5)
    s5_c_im = nrm((ns, SSM_GROUPS, SSM_GROUP, SSM_STATE), (2 * SSM_STATE) ** -0.5)
    s5_d = nrm((ns, d), 1.0)
    s5_w_glu = nrm((ns, d, 2 * d), d ** -0.5)
    moe_norm_g = gain((ns, d))
    moe_w_router = nrm((ns, d, N_EXPERTS), d ** -0.5)
    moe_b_router = nrm((ns, N_EXPERTS), 0.01)
    moe_w_gate = nrm((ns, N_EXPERTS, d, D_FF_EXPERT), d ** -0.5)
    moe_w_up = nrm((ns, N_EXPERTS, d, D_FF_EXPERT), d ** -0.5)
    moe_w_down = nrm((ns, N_EXPERTS, D_FF_EXPERT, d), D_FF_EXPERT ** -0.5)
    return {
        "x_prompt": x_prompt, "x_sample": x_sample,
        "cache_k": cache_k, "cache_v": cache_v, "cache_kidx": cache_kidx,
        "state_s5_re": state_s5_re, "state_s5_im": state_s5_im, "page_table": page_table,
        "attn_norm_g": attn_norm_g, "attn_w_in": attn_w_in, "attn_g_q": attn_g_q,
        "attn_g_k": attn_g_k, "attn_g_kidx": attn_g_kidx, "attn_w_o": attn_w_o,
        "dense_norm_g": dense_norm_g, "dense_w_gate": dense_w_gate, "dense_w_up": dense_w_up,
        "dense_w_down": dense_w_down,
        "s5_norm_g": s5_norm_g, "s5_w_in": s5_w_in, "s5_a_re": s5_a_re, "s5_a_im": s5_a_im,
        "s5_log_dt": s5_log_dt, "s5_b_re": s5_b_re, "s5_b_im": s5_b_im, "s5_c_re": s5_c_re,
        "s5_c_im": s5_c_im, "s5_d": s5_d, "s5_w_glu": s5_w_glu,
        "moe_norm_g": moe_norm_g, "moe_w_router": moe_w_router, "moe_b_router": moe_b_router,
        "moe_w_gate": moe_w_gate, "moe_w_up": moe_w_up, "moe_w_down": moe_w_down,
    }


def reference(x_prompt, x_sample, cache_k, cache_v, cache_kidx, state_s5_re, state_s5_im, page_table,
              attn_norm_g, attn_w_in, attn_g_q, attn_g_k, attn_g_kidx, attn_w_o,
              dense_norm_g, dense_w_gate, dense_w_up, dense_w_down,
              s5_norm_g, s5_w_in, s5_a_re, s5_a_im, s5_log_dt, s5_b_re, s5_b_im, s5_c_re, s5_c_im,
              s5_d, s5_w_glu,
              moe_norm_g, moe_w_router, moe_b_router, moe_w_gate, moe_w_up, moe_w_down):
    hp, hs = x_prompt, x_sample
    kp_l, vp_l, kip_l, srp_l, sip_l = [], [], [], [], []
    ks_l, vs_l, kis_l, srs_l, sis_l = [], [], [], [], []
    for i in range(DEPTH):
        j = i // 2
        if i % 2 == 0:
            ap = (attn_norm_g[j], attn_w_in[j], attn_g_q[j], attn_g_k[j], attn_g_kidx[j], attn_w_o[j])
            dp, kp, vp, kip = attn_prompt(hp, *ap)
            ds, ks, vs, kis = attn_sample(hs, cache_k[j], cache_v[j], cache_kidx[j], page_table, *ap)
            hp = hp + dp
            hs = hs + ds
            fp = (dense_norm_g[j], dense_w_gate[j], dense_w_up[j], dense_w_down[j])
            hp = hp + swiglu_ffn(hp, *fp)
            hs = hs + swiglu_ffn(hs, *fp)
            kp_l.append(kp); vp_l.append(vp); kip_l.append(kip)
            ks_l.append(ks); vs_l.append(vs); kis_l.append(kis)
        else:
            sp = (s5_norm_g[j], s5_w_in[j], s5_a_re[j], s5_a_im[j], s5_log_dt[j], s5_b_re[j], s5_b_im[j],
                  s5_c_re[j], s5_c_im[j], s5_d[j], s5_w_glu[j])
            dp, srp, sip = s5_layer(hp, None, None, *sp)
            ds, srs, sis = s5_layer(hs, state_s5_re[j], state_s5_im[j], *sp)
            hp = hp + dp
            hs = hs + ds
            mp = (moe_norm_g[j], moe_w_router[j], moe_b_router[j], moe_w_gate[j], moe_w_up[j], moe_w_down[j])
            hp = hp + moe_ffn(hp, *mp)
            hs = hs + moe_ffn(hs, *mp)
            srp_l.append(srp); sip_l.append(sip)
            srs_l.append(srs); sis_l.append(sis)
    return (hp, hs,
            jnp.stack(kp_l), jnp.stack(vp_l), jnp.stack(kip_l), jnp.stack(srp_l), jnp.stack(sip_l),
            jnp.stack(ks_l), jnp.stack(vs_l), jnp.stack(kis_l), jnp.stack(srs_l), jnp.stack(sis_l))
```

```python
import functools
import math

import jax
import jax.numpy as jnp
import numpy as np
from jax import lax
from jax.experimental import pallas as pl
from jax.experimental.pallas import tpu as pltpu

F32 = jnp.float32
BF16 = jnp.bfloat16
I32 = jnp.int32

D_MODEL = 1024
N_HEADS = 16
N_KV_HEADS = 4
HEAD_DIM = 64
Q_PER_KV = N_HEADS // N_KV_HEADS
IDX_HEADS = 16
IDX_DIM = 64
TOPK_MAX = 256
PAGE_SIZE = 128
ATTN_SCALE = HEAD_DIM ** -0.5
INDEX_SCALE = (IDX_HEADS ** -0.5) * (IDX_DIM ** -0.5)
Q_COLS = N_HEADS * HEAD_DIM
KV_COLS = N_KV_HEADS * HEAD_DIM
QI_COLS = IDX_HEADS * IDX_DIM
SSM_GROUP = 16
SSM_GROUPS = D_MODEL // SSM_GROUP
SSM_STATE = 64
SSM_COLS = SSM_GROUPS * SSM_STATE
N_EXPERTS = 8
RMS_EPS = 1e-6

LANES = 128
MXU_DIM = 256
VMEM_LIMIT = 56 * 1024 * 1024
INT_MIN = -(2 ** 31)
NEG_INF_KEY = INT_MIN + 0x7FFFFF

ROW_TILE = 512
ATTN_Q_TILE = 128
ATTN_KEY_BUCKET = 512
S5_TIME_TILE = 32
S5_SCAN_ELEMS = 8192
MOE_ROW_TILE = 256
MOE_DMA_CHUNK = 256


def _cparams(*sem):
    return pltpu.CompilerParams(dimension_semantics=sem, vmem_limit_bytes=VMEM_LIMIT)


def _rms(x, g):
    return x * lax.rsqrt(jnp.mean(x * x, axis=-1, keepdims=True) + RMS_EPS) * g


def _dot(a, b):
    return jnp.dot(a, b, preferred_element_type=F32)


def _dot_nt(a, b):
    return lax.dot_general(a, b, (((1,), (1,)), ((), ())), preferred_element_type=F32)


def _full(shape):
    n = len(shape)
    return pl.BlockSpec(shape, lambda *_: (0,) * n)


def _attn_proj_kernel(x_ref, ng_ref, wq_ref, wkv_ref, wqi_ref, wkw_ref, gk_ref, gki_ref,
                      q_ref, qi_ref, wi_ref, k_ref, v_ref, ki_ref, *rest, transposed):
    xn = _rms(x_ref[...], ng_ref[...]).astype(BF16)
    q_ref[...] = _dot_nt(xn, wq_ref[...])
    qi_ref[...] = _dot_nt(xn, wqi_ref[...]).astype(BF16)
    if transposed:
        kb_ref, vb_ref, kib_ref = rest
        tm = xn.shape[0]
        kv = _dot_nt(wkv_ref[...], xn)
        k = kv[:KV_COLS].reshape(N_KV_HEADS, HEAD_DIM, tm)
        k = k * lax.rsqrt(jnp.mean(k * k, axis=1, keepdims=True) + RMS_EPS) * gk_ref[...][None]
        k = k.reshape(KV_COLS, tm)
        v = kv[KV_COLS:]
        kw = _dot_nt(wkw_ref[...], xn)
        ki = kw[:IDX_DIM]
        ki = ki * lax.rsqrt(jnp.mean(ki * ki, axis=0, keepdims=True) + RMS_EPS) * gki_ref[...]
        wi_ref[...] = kw[IDX_DIM:IDX_DIM + IDX_HEADS].T
        k_ref[0] = k
        v_ref[0] = v
        ki_ref[0] = ki
        kb_ref[0] = k.astype(BF16)
        vb_ref[0] = v.astype(BF16)
        kib_ref[0] = ki.astype(BF16)
    else:
        kv = _dot_nt(xn, wkv_ref[...])
        gk = gk_ref[...]
        ks = []
        for h in range(N_KV_HEADS):
            seg = kv[:, h * HEAD_DIM:(h + 1) * HEAD_DIM]
            ks.append(_rms(seg, gk))
        k_ref[...] = jnp.concatenate(ks, axis=-1)
        v_ref[...] = kv[:, KV_COLS:]
        kw = _dot_nt(xn, wkw_ref[...])
        ki_ref[...] = _rms(kw[:, :IDX_DIM], gki_ref[...])
        wi_ref[...] = kw[:, IDX_DIM:IDX_DIM + IDX_HEADS]


def _attn_proj(x2d, norm_g, w_parts, g_k, g_kidx, *, seq_len=None):
    m = x2d.shape[0]
    tm = min(ROW_TILE, m)
    wq, wkv, wqi, wkw = w_parts
    transposed = seq_len is not None
    row = lambda cols: pl.BlockSpec((tm, cols), lambda i: (i, 0))
    in_specs = [row(D_MODEL), _full((1, D_MODEL)), _full(wq.shape), _full(wkv.shape), _full(wqi.shape),
                _full(wkw.shape)]
    out_shape = [jax.ShapeDtypeStruct((m, Q_COLS), F32), jax.ShapeDtypeStruct((m, QI_COLS), BF16),
                 jax.ShapeDtypeStruct((m, IDX_HEADS), F32)]
    out_specs = [row(Q_COLS), row(QI_COLS), row(IDX_HEADS)]
    if transposed:
        nb, per = m // seq_len, seq_len // tm
        fm = lambda rows: pl.BlockSpec((1, rows, tm), lambda i: (i // per, 0, i % per))
        in_specs += [_full((HEAD_DIM, 1)), _full((IDX_DIM, 1))]
        gk_in, gki_in = g_k.reshape(HEAD_DIM, 1), g_kidx.reshape(IDX_DIM, 1)
        for dt in (F32, BF16):
            out_shape += [jax.ShapeDtypeStruct((nb, KV_COLS, seq_len), dt),
                          jax.ShapeDtypeStruct((nb, KV_COLS, seq_len), dt),
                          jax.ShapeDtypeStruct((nb, IDX_DIM, seq_len), dt)]
            out_specs += [fm(KV_COLS), fm(KV_COLS), fm(IDX_DIM)]
    else:
        in_specs += [_full((1, HEAD_DIM)), _full((1, IDX_DIM))]
        gk_in, gki_in = g_k.reshape(1, HEAD_DIM), g_kidx.reshape(1, IDX_DIM)
        out_shape += [jax.ShapeDtypeStruct((m, KV_COLS), F32), jax.ShapeDtypeStruct((m, KV_COLS), F32),
                      jax.ShapeDtypeStruct((m, IDX_DIM), F32)]
        out_specs += [row(KV_COLS), row(KV_COLS), row(IDX_DIM)]
    return pl.pallas_call(
        functools.partial(_attn_proj_kernel, transposed=transposed),
        grid=(m // tm,), in_specs=in_specs, out_specs=out_specs, out_shape=out_shape,
        compiler_params=_cparams("parallel"), name="attn_proj",
    )(x2d, norm_g.reshape(1, D_MODEL), wq, wkv, wqi, wkw, gk_in, gki_in)


def _sortable_key(score):
    score = jnp.where(score == 0.0, 0.0, score)
    bits = pltpu.bitcast(score, I32)
    return bits ^ (lax.shift_right_arithmetic(bits, 31) & 0x7FFFFFFF)


def _count(mask):
    return jnp.sum(jnp.where(mask, 1.0, 0.0), axis=-1, keepdims=True)


def _topk_select(key, pos, k, lim_ref, pos_bits):
    r = key.shape[0]
    kf = float(k)
    thr0 = jnp.where(_count(key >= 0) >= kf, 0, INT_MIN).astype(I32)

    def value_bit(i, thr):
        cand = thr | lax.shift_left(jnp.int32(1), 30 - i)
        return jnp.where(_count(key >= cand) >= kf, cand, thr)

    thr = lax.fori_loop(0, 31, value_bit, thr0)
    above = key > thr
    tied = key == thr
    take = kf - _count(above)
    lim_ref[...] = jnp.full((r, 1), 2 ** pos_bits, I32)
    excess = (_count(tied) > take) & (thr > NEG_INF_KEY)

    @pl.when(jnp.max(jnp.where(excess, 1.0, 0.0)) > 0.0)
    def _():
        def pos_bit(i, lo):
            cand = lo | lax.shift_left(jnp.int32(1), pos_bits - 1 - i)
            return jnp.where(_count(tied & (pos < cand)) < take, cand, lo)

        lim_ref[...] = lax.fori_loop(0, pos_bits, pos_bit, jnp.zeros((r, 1), I32))

    return above | (tied & (pos <= lim_ref[...]))


def _norm_q_heads(q, gq):
    return [_rms(q[:, h * HEAD_DIM:(h + 1) * HEAD_DIM], gq) * ATTN_SCALE for h in range(N_HEADS)]


def _attn_prompt_kernel(q_ref, qi_ref, wi_ref, kt_ref, vt_ref, kit_ref, gq_ref, o_ref, lim_ref, *, tq, ksel):
    j = pl.program_id(1)
    seq = kt_ref.shape[2]
    n_buckets = seq // ATTN_KEY_BUCKET
    per_bucket = ATTN_KEY_BUCKET // tq

    def block(s):
        q_pos = j * tq + lax.broadcasted_iota(I32, (tq, 1), 0)
        key_pos = lax.broadcasted_iota(I32, (1, s), 1)
        causal = key_pos <= q_pos
        qi = qi_ref[...]
        wi = wi_ref[...]
        kit = kit_ref[0, :, :s]
        score = jnp.zeros((tq, s), F32)
        for h in range(IDX_HEADS):
            sh = _dot(qi[:, h * IDX_DIM:(h + 1) * IDX_DIM], kit)
            score = score + wi[:, h:h + 1] * jnp.maximum(sh, 0.0)
        score = jnp.where(causal, score * INDEX_SCALE, -jnp.inf)
        sel = _topk_select(_sortable_key(score), key_pos, ksel, lim_ref, pos_bits=int(math.log2(seq)))
        bias = jnp.where(sel & causal, 0.0, -jnp.inf)[None]

        heads = _norm_q_heads(q_ref[...], gq_ref[...])
        outs = [None] * N_HEADS
        for n in range(N_KV_HEADS):
            qn = jnp.concatenate(heads[n * Q_PER_KV:(n + 1) * Q_PER_KV], axis=0).astype(BF16)
            kn = kt_ref[0, n * HEAD_DIM:(n + 1) * HEAD_DIM, :s]
            vn = vt_ref[0, n * HEAD_DIM:(n + 1) * HEAD_DIM, :s]
            logits = _dot(qn, kn).reshape(Q_PER_KV, tq, s) + bias
            m = jnp.max(logits, axis=-1, keepdims=True)
            p = jnp.exp(logits - m)
            l = jnp.sum(p, axis=-1, keepdims=True)
            pv = _dot_nt(p.reshape(Q_PER_KV * tq, s).astype(BF16), vn)
            pv = pv / l.reshape(Q_PER_KV * tq, 1)
            for g in range(Q_PER_KV):
                outs[n * Q_PER_KV + g] = pv[g * tq:(g + 1) * tq]
        o_ref[...] = jnp.concatenate(outs, axis=-1).astype(BF16)

    for e in range(n_buckets):
        pl.when(j // per_bucket == e)(functools.partial(block, (e + 1) * ATTN_KEY_BUCKET))


def _attn_prompt(q, qi, wi, ktb, vtb, kitb, g_q):
    nb, _, seq = ktb.shape
    tq = ATTN_Q_TILE
    nq = seq // tq
    ksel = min(TOPK_MAX, seq // 4)
    row = lambda cols: pl.BlockSpec((tq, cols), lambda b, j: (b * nq + j, 0))
    fm = lambda rows: pl.BlockSpec((1, rows, seq), lambda b, j: (b, 0, 0))
    return pl.pallas_call(
        functools.partial(_attn_prompt_kernel, tq=tq, ksel=ksel),
        grid=(nb, nq),
        in_specs=[row(Q_COLS), row(QI_COLS), row(IDX_HEADS), fm(KV_COLS), fm(KV_COLS), fm(IDX_DIM),
                  _full((1, HEAD_DIM))],
        out_specs=row(Q_COLS),
        out_shape=jax.ShapeDtypeStruct((nb * seq, Q_COLS), BF16),
        scratch_shapes=[pltpu.VMEM((tq, 1), I32)],
        compiler_params=_cparams("parallel", "arbitrary"), name="attn_prompt",
    )(q, qi, wi, ktb, vtb, kitb, g_q.reshape(1, HEAD_DIM))


def _attn_sample_kernel(pt_ref, q_ref, qi_ref, wi_ref, kn_ref, vn_ref, kin_ref, gq_ref,
                        ck_hbm, cv_hbm, cki_hbm, o_ref, kbuf, vbuf, kibuf, sems, lim_ref,
                        *, n_pages, t_new, ksel):
    b = pl.program_id(0)
    nb = pl.num_programs(0)
    past = n_pages * PAGE_SIZE
    tp = q_ref.shape[1]

    def page_copies(batch, slot, p):
        page = pt_ref[batch, p]
        cols = pl.ds(p * PAGE_SIZE, PAGE_SIZE)
        return (pltpu.make_async_copy(ck_hbm.at[page], kbuf.at[slot, :, cols], sems.at[0, slot]),
                pltpu.make_async_copy(cv_hbm.at[page], vbuf.at[slot, :, cols], sems.at[1, slot]),
                pltpu.make_async_copy(cki_hbm.at[page], kibuf.at[slot, :, cols], sems.at[2, slot]))

    def fetch(batch, slot):
        def issue(p, c):
            for cp in page_copies(batch, slot, p):
                cp.start()
            return c
        lax.fori_loop(0, n_pages, issue, 0)

    def wait(batch, slot):
        def done(p, c):
            for cp in page_copies(batch, slot, p):
                cp.wait()
            return c
        lax.fori_loop(0, n_pages, done, 0)

    slot = b % 2

    @pl.when(b == 0)
    def _():
        fetch(0, 0)

    @pl.when(b + 1 < nb)
    def _():
        fetch(b + 1, 1 - slot)

    wait(b, slot)

    nrow = N_HEADS * tp
    s_all = past + LANES

    def new_rows(ref):
        x = ref[0]
        return jnp.concatenate([x, jnp.zeros((LANES - tp, x.shape[1]), F32)], axis=0).astype(BF16)

    qi = qi_ref[0]
    wi = wi_ref[0]
    qi_rows = jnp.concatenate([qi[:, h * IDX_DIM:(h + 1) * IDX_DIM] for h in range(IDX_HEADS)],
                              axis=0).astype(BF16)
    wi_rows = jnp.concatenate([wi[:, h:h + 1] for h in range(IDX_HEADS)], axis=0)
    s_idx = jnp.concatenate([_dot(qi_rows, kibuf[slot].astype(BF16)),
                             _dot_nt(qi_rows, new_rows(kin_ref))], axis=1)
    s_idx = wi_rows * jnp.maximum(s_idx, 0.0)
    score = jnp.sum(s_idx.reshape(IDX_HEADS, tp, s_all), axis=0) * INDEX_SCALE
    q_pos = past + lax.broadcasted_iota(I32, (tp, 1), 0)
    key_pos = lax.broadcasted_iota(I32, (1, s_all), 1)
    causal = (key_pos <= q_pos) & (key_pos < past + t_new)
    score = jnp.where(causal, score, -jnp.inf)
    sel = _topk_select(_sortable_key(score), key_pos, ksel, lim_ref, pos_bits=int(math.ceil(math.log2(s_all))))
    bias = jnp.where(sel & causal, 0.0, -jnp.inf)[None]

    heads = _norm_q_heads(q_ref[0], gq_ref[...])
    zero = jnp.zeros((Q_PER_KV * tp, HEAD_DIM), F32)
    qbd = jnp.concatenate([
        jnp.concatenate([jnp.concatenate(heads[n * Q_PER_KV:(n + 1) * Q_PER_KV], axis=0) if c == n else zero
                         for c in range(N_KV_HEADS)], axis=1)
        for n in range(N_KV_HEADS)], axis=0).astype(BF16)
    logits = jnp.concatenate([_dot(qbd, kbuf[slot].astype(BF16)),
                              _dot_nt(qbd, new_rows(kn_ref))], axis=1)
    logits = (logits.reshape(N_HEADS, tp, s_all) + bias).reshape(nrow, s_all)
    m = jnp.max(logits, axis=-1, keepdims=True)
    p = jnp.exp(logits - m)
    l = jnp.sum(p, axis=-1, keepdims=True)
    pb = p.astype(BF16)
    pv = _dot_nt(pb[:, :past], vbuf[slot].astype(BF16)) + _dot(pb[:, past:], new_rows(vn_ref))
    pv = pv / l
    outs = []
    for n in range(N_KV_HEADS):
        for g in range(Q_PER_KV):
            r0 = (n * Q_PER_KV + g) * tp
            outs.append(pv[r0:r0 + tp, n * HEAD_DIM:(n + 1) * HEAD_DIM])
    o_ref[0] = jnp.concatenate(outs, axis=-1).astype(BF16)


def _attn_sample(q, qi, wi, k_new, v_new, ki_new, g_q, page_table, ck, cv, cki, *, t_new):
    nb, tp, _ = q.shape
    n_pages = page_table.shape[1]
    past = n_pages * PAGE_SIZE
    ksel = min(TOPK_MAX, (past + t_new) // 4)
    blk = lambda cols: pl.BlockSpec((1, tp, cols), lambda b, pt: (b, 0, 0))
    grid_spec = pltpu.PrefetchScalarGridSpec(
        num_scalar_prefetch=1, grid=(nb,),
        in_specs=[blk(Q_COLS), blk(QI_COLS), blk(IDX_HEADS), blk(KV_COLS), blk(KV_COLS), blk(IDX_DIM),
                  pl.BlockSpec((1, HEAD_DIM), lambda b, pt: (0, 0)),
                  pl.BlockSpec(memory_space=pl.ANY), pl.BlockSpec(memory_space=pl.ANY),
                  pl.BlockSpec(memory_space=pl.ANY)],
        out_specs=blk(Q_COLS),
        scratch_shapes=[pltpu.VMEM((2, KV_COLS, past), F32), pltpu.VMEM((2, KV_COLS, past), F32),
                        pltpu.VMEM((2, IDX_DIM, past), F32), pltpu.SemaphoreType.DMA((3, 2)),
                        pltpu.VMEM((tp, 1), I32)])
    return pl.pallas_call(
        functools.partial(_attn_sample_kernel, n_pages=n_pages, t_new=t_new, ksel=ksel),
        grid_spec=grid_spec, out_shape=jax.ShapeDtypeStruct((nb, tp, Q_COLS), BF16),
        compiler_params=_cparams("arbitrary"), name="attn_sample",
    )(page_table, q, qi, wi, k_new, v_new, ki_new, g_q.reshape(1, HEAD_DIM), ck, cv, cki)


def _attn_out_ffn_kernel(x_ref, o_ref, wo_ref, ng_ref, wg_ref, wu_ref, wd_ref, out_ref, h_ref, xn_ref, acc_ref):
    j = pl.program_id(1)

    @pl.when(j == 0)
    def _():
        h = x_ref[...] + _dot(o_ref[...], wo_ref[...])
        h_ref[...] = h
        xn_ref[...] = _rms(h, ng_ref[...]).astype(BF16)
        acc_ref[...] = jnp.zeros_like(acc_ref)

    xn = xn_ref[...]
    gate = _dot(xn, wg_ref[...])
    up = _dot(xn, wu_ref[...])
    act = (gate * jax.nn.sigmoid(gate) * up).astype(BF16)
    acc_ref[...] += _dot(act, wd_ref[...])

    @pl.when(j == pl.num_programs(1) - 1)
    def _():
        out_ref[...] = h_ref[...] + acc_ref[...]


def _attn_out_ffn(x2d, o, w_o, norm_g, w_gate, w_up, w_down):
    m = x2d.shape[0]
    tm = min(ROW_TILE, m)
    d_ff = w_gate.shape[1]
    tf = d_ff // 2
    row = lambda cols: pl.BlockSpec((tm, cols), lambda i, j: (i, 0))
    return pl.pallas_call(
        _attn_out_ffn_kernel, grid=(m // tm, d_ff // tf),
        in_specs=[row(D_MODEL), row(Q_COLS), pl.BlockSpec(w_o.shape, lambda i, j: (0, 0)),
                  pl.BlockSpec((1, D_MODEL), lambda i, j: (0, 0)),
                  pl.BlockSpec((D_MODEL, tf), lambda i, j: (0, j)),
                  pl.BlockSpec((D_MODEL, tf), lambda i, j: (0, j)),
                  pl.BlockSpec((tf, D_MODEL), lambda i, j: (j, 0))],
        out_specs=row(D_MODEL), out_shape=jax.ShapeDtypeStruct((m, D_MODEL), F32),
        scratch_shapes=[pltpu.VMEM((tm, D_MODEL), F32), pltpu.VMEM((tm, D_MODEL), BF16),
                        pltpu.VMEM((tm, D_MODEL), F32)],
        compiler_params=_cparams("parallel", "arbitrary"), name="attn_out_ffn",
    )(x2d, o, w_o, norm_g.reshape(1, D_MODEL), w_gate, w_up, w_down)


def _s5_params_kernel(are_ref, aim_ref, ldt_ref, bre_ref, bim_ref, abr_ref, abi_ref, bbr_ref, bbi_ref):
    dt = jnp.exp(ldt_ref[...])
    lr = are_ref[...] * dt
    li = aim_ref[...] * dt
    mag = jnp.exp(lr)
    abr = mag * jnp.cos(li)
    abi = mag * jnp.sin(li)
    den = lr * lr + li * li
    fr = ((abr - 1.0) * lr + abi * li) / den * dt
    fi = (abi * lr - (abr - 1.0) * li) / den * dt
    abr_ref[...] = abr
    abi_ref[...] = abi
    bre = bre_ref[...]
    bim = bim_ref[...]
    bbr_ref[...] = fr * bre - fi * bim
    bbi_ref[...] = fr * bim + fi * bre


def _s5_params(a_re, a_im, log_dt, b_re, b_im):
    g, p, c = b_re.shape
    bt = lambda b: jnp.transpose(b, (0, 2, 1))
    shp3 = jax.ShapeDtypeStruct((g, 1, p), F32)
    shpb = jax.ShapeDtypeStruct((g, c, p), F32)
    return pl.pallas_call(
        _s5_params_kernel, out_shape=[shp3, shp3, shpb, shpb], name="s5_params",
    )(a_re.reshape(g, 1, p), a_im.reshape(g, 1, p), log_dt.reshape(g, 1, 1), bt(b_re), bt(b_im))


def _block_diag_tiles(w, groups_per_tile):
    g, a, b = w.shape
    eye = jnp.eye(groups_per_tile, dtype=w.dtype)
    w = w.reshape(g // groups_per_tile, groups_per_tile, a, b)
    out = w[:, :, :, None, :] * eye[None, :, None, :, None]
    return out.reshape(g // groups_per_tile, groups_per_tile * a, groups_per_tile * b)


def _s5_core_kernel(x_ref, ng_ref, win_ref, perm_ref, permt_ref, bbr_ref, bbi_ref, cr_ref, ci_ref,
                    abr_ref, abi_ref, d_ref, s0r_ref, s0i_ref, z_ref, sr_ref, si_ref, xr_ref, xi_ref,
                    *, nb, tt):
    i = pl.program_id(0)
    rows = nb * tt
    gpt = MXU_DIM // SSM_GROUP
    n_kt = D_MODEL // MXU_DIM
    st_cols = gpt * SSM_STATE

    @pl.when(i == 0)
    def _():
        sr_ref[...] = s0r_ref[...]
        si_ref[...] = s0i_ref[...]

    x = x_ref[...].reshape(rows, D_MODEL)
    xn = _rms(x, ng_ref[...]).astype(BF16)
    xn = _dot(perm_ref[...], xn).astype(BF16)
    u = _dot(xn, win_ref[...])
    ub = u.astype(BF16)
    for kt in range(n_kt):
        uk = ub[:, kt * MXU_DIM:(kt + 1) * MXU_DIM]
        xr_ref[:, kt * st_cols:(kt + 1) * st_cols] = _dot(uk, bbr_ref[kt])
        xi_ref[:, kt * st_cols:(kt + 1) * st_cols] = _dot(uk, bbi_ref[kt])

    cw = S5_SCAN_ELEMS // nb
    for c0 in range(0, SSM_COLS, cw):
        cols = pl.ds(c0, cw)
        ar = jnp.broadcast_to(abr_ref[:, cols], (nb, cw))
        ai = jnp.broadcast_to(abi_ref[:, cols], (nb, cw))

        def step(t, carry):
            hr, hi = carry
            r = pl.ds(pl.multiple_of(t * nb, nb), nb)
            nr = ar * hr - ai * hi + xr_ref[r, cols]
            ni = ar * hi + ai * hr + xi_ref[r, cols]
            xr_ref[r, cols] = nr
            xi_ref[r, cols] = ni
            return nr, ni

        hr, hi = lax.fori_loop(0, tt, step, (sr_ref[:, cols], si_ref[:, cols]))
        sr_ref[:, cols] = hr
        si_ref[:, cols] = hi

    ys = []
    for kt in range(n_kt):
        sc = pl.ds(kt * st_cols, st_cols)
        ys.append(_dot(xr_ref[:, sc].astype(BF16), cr_ref[kt]) - _dot(xi_ref[:, sc].astype(BF16), ci_ref[kt]))
    y = jnp.concatenate(ys, axis=-1) + d_ref[...] * u
    z = jax.nn.gelu(y).astype(BF16)
    z_ref[...] = _dot(permt_ref[...], z).astype(BF16).reshape(z_ref.shape)


def _s5_core(x, s0_re, s0_im, norm_g, w_in, abr, abi, bbr, bbi, c_re, c_im, d_skip, *, tt):
    nb, t, _ = x.shape
    rows = nb * tt
    gpt = MXU_DIM // SSM_GROUP
    r = np.arange(rows)
    perm_np = np.zeros((rows, rows), np.float32)
    perm_np[r, (r % nb) * tt + r // nb] = 1.0
    perm = jnp.asarray(perm_np, BF16)
    permt = jnp.asarray(perm_np.T, BF16)
    bbr_t = _block_diag_tiles(bbr, gpt).astype(BF16)
    bbi_t = _block_diag_tiles(bbi, gpt).astype(BF16)
    ct = lambda c: _block_diag_tiles(jnp.transpose(c, (0, 2, 1)), gpt).astype(BF16)
    if tt % 8 == 0:
        x_in, x_spec = x, pl.BlockSpec((nb, tt, D_MODEL), lambda i: (0, i, 0))
    else:
        assert tt == t
        x_in, x_spec = x.reshape(rows, D_MODEL), pl.BlockSpec((rows, D_MODEL), lambda i: (0, 0))
    z_spec = (pl.BlockSpec((nb, tt, D_MODEL), lambda i: (0, i, 0)) if tt % 8 == 0
              else pl.BlockSpec((rows, D_MODEL), lambda i: (0, 0)))
    z_shape = (nb, t, D_MODEL) if tt % 8 == 0 else (rows, D_MODEL)
    st = jax.ShapeDtypeStruct((nb, SSM_COLS), F32)
    z, sr, si = pl.pallas_call(
        functools.partial(_s5_core_kernel, nb=nb, tt=tt), grid=(t // tt,),
        in_specs=[x_spec, _full((1, D_MODEL)), _full(w_in.shape), _full(perm.shape), _full(permt.shape),
                  _full(bbr_t.shape), _full(bbi_t.shape), _full((D_MODEL // MXU_DIM, gpt * SSM_STATE, MXU_DIM)),
                  _full((D_MODEL // MXU_DIM, gpt * SSM_STATE, MXU_DIM)),
                  _full((1, SSM_COLS)), _full((1, SSM_COLS)), _full((1, D_MODEL)),
                  _full((nb, SSM_COLS)), _full((nb, SSM_COLS))],
        out_specs=[z_spec, _full((nb, SSM_COLS)), _full((nb, SSM_COLS))],
        out_shape=[jax.ShapeDtypeStruct(z_shape, BF16), st, st],
        scratch_shapes=[pltpu.VMEM((rows, SSM_COLS), F32), pltpu.VMEM((rows, SSM_COLS), F32)],
        compiler_params=_cparams("arbitrary"), name="s5_core",
    )(x_in, norm_g.reshape(1, D_MODEL), w_in, perm, permt, bbr_t, bbi_t, ct(c_re), ct(c_im),
      abr.reshape(1, SSM_COLS), abi.reshape(1, SSM_COLS), d_skip.reshape(1, D_MODEL), s0_re, s0_im)
    return z.reshape(nb * t, D_MODEL), sr, si


def _glu_kernel(h_ref, z_ref, w_ref, out_ref):
    gl = _dot(z_ref[...], w_ref[...])
    out_ref[...] = h_ref[...] + gl[:, :D_MODEL] * jax.nn.sigmoid(gl[:, D_MODEL:])


def _glu_residual(h2d, z, w_glu):
    m = h2d.shape[0]
    tm = min(ROW_TILE, m)
    row = pl.BlockSpec((tm, D_MODEL), lambda i: (i, 0))
    return pl.pallas_call(
        _glu_kernel, grid=(m // tm,), in_specs=[row, row, _full(w_glu.shape)], out_specs=row,
        out_shape=jax.ShapeDtypeStruct((m, D_MODEL), F32), compiler_params=_cparams("parallel"), name="s5_glu",
    )(h2d, z, w_glu)


def _moe_router_kernel(h_ref, ng_ref, wr_ref, br_ref, xn_ref, idx_ref, gate_ref, rank_ref, cnt_ref, carry_ref):
    i = pl.program_id(0)
    tm = h_ref.shape[0]

    @pl.when(i == 0)
    def _():
        carry_ref[...] = jnp.zeros_like(carry_ref)

    xn = _rms(h_ref[...], ng_ref[...])
    xn_ref[...] = xn
    logits = lax.dot_general(wr_ref[...], xn, (((1,), (1,)), ((), ())), preferred_element_type=F32,
                             precision=lax.Precision.HIGHEST) + br_ref[...]
    eid = lax.broadcasted_iota(I32, (N_EXPERTS, tm), 0).astype(F32)
    m1 = jnp.max(logits, axis=0, keepdims=True)
    i1 = jnp.min(jnp.where(logits == m1, eid, float(N_EXPERTS)), axis=0, keepdims=True)
    rest = jnp.where(eid == i1, -jnp.inf, logits)
    m2 = jnp.max(rest, axis=0, keepdims=True)
    i2 = jnp.min(jnp.where(rest == m2, eid, float(N_EXPERTS)), axis=0, keepdims=True)
    e2 = jnp.exp(m2 - m1)
    den = 1.0 + e2
    idx_ref[...] = jnp.concatenate([i1, i2], axis=0).astype(I32)
    gate_ref[...] = jnp.concatenate([1.0 / den, e2 / den], axis=0)

    onehot = jnp.where((eid == i1) | (eid == i2), 1.0, 0.0)
    src = lax.broadcasted_iota(I32, (tm, tm), 0)
    dst = lax.broadcasted_iota(I32, (tm, tm), 1)
    tri = jnp.where(src <= dst, 1.0, 0.0).astype(BF16)
    cum = _dot(onehot.astype(BF16), tri) + carry_ref[:, 0:1]
    before = cum - 1.0
    r1 = jnp.sum(jnp.where(eid == i1, before, 0.0), axis=0, keepdims=True)
    r2 = jnp.sum(jnp.where(eid == i2, before, 0.0), axis=0, keepdims=True)
    rank_ref[...] = jnp.concatenate([r1, r2], axis=0).astype(I32)
    total = cum[:, tm - 1:tm]
    carry_ref[...] = jnp.broadcast_to(total, carry_ref.shape)
    cnt_ref[...] = jnp.broadcast_to(total, cnt_ref.shape).astype(I32)


def _moe_router(h2d, norm_g, w_router, b_router):
    m = h2d.shape[0]
    tm = min(ROW_TILE, m)
    row = pl.BlockSpec((tm, D_MODEL), lambda i: (i, 0))
    col2 = pl.BlockSpec((2, tm), lambda i: (0, i))
    return pl.pallas_call(
        _moe_router_kernel, grid=(m // tm,),
        in_specs=[row, _full((1, D_MODEL)), _full((N_EXPERTS, D_MODEL)), _full((N_EXPERTS, 1))],
        out_specs=[row, col2, col2, col2, _full((N_EXPERTS, LANES))],
        out_shape=[jax.ShapeDtypeStruct((m, D_MODEL), F32), jax.ShapeDtypeStruct((2, m), I32),
                   jax.ShapeDtypeStruct((2, m), F32), jax.ShapeDtypeStruct((2, m), I32),
                   jax.ShapeDtypeStruct((N_EXPERTS, LANES), I32)],
        scratch_shapes=[pltpu.VMEM((N_EXPERTS, LANES), F32)],
        compiler_params=_cparams("arbitrary"), name="moe_router",
    )(h2d, norm_g.reshape(1, D_MODEL), w_router.T, b_router.reshape(N_EXPERTS, 1))


def _moe_gather_kernel(pos_ref, xn_hbm, xs_in_hbm, xs_hbm, sem):
    del xs_in_hbm
    i = pl.program_id(0)
    ch = pos_ref.shape[1]

    def copy(t, k):
        return pltpu.make_async_copy(xn_hbm.at[pl.ds(i * ch + t, 1)], xs_hbm.at[pl.ds(pos_ref[k, t], 1)], sem)

    def issue(t, c):
        copy(t, 0).start()
        copy(t, 1).start()
        return c

    def drain(t, c):
        copy(t, 0).wait()
        copy(t, 1).wait()
        return c

    lax.fori_loop(0, ch, issue, 0)
    lax.fori_loop(0, ch, drain, 0)


def _moe_gather(xn, pos, n_rows):
    m = xn.shape[0]
    ch = min(MOE_DMA_CHUNK, m)
    xs0 = jnp.zeros((n_rows, D_MODEL), F32)
    return pl.pallas_call(
        _moe_gather_kernel, grid=(m // ch,),
        in_specs=[pl.BlockSpec((2, ch), lambda i: (0, i), memory_space=pltpu.SMEM),
                  pl.BlockSpec(memory_space=pl.ANY), pl.BlockSpec(memory_space=pl.ANY)],
        out_specs=pl.BlockSpec(memory_space=pl.ANY),
        out_shape=jax.ShapeDtypeStruct((n_rows, D_MODEL), F32),
        scratch_shapes=[pltpu.SemaphoreType.DMA(())],
        input_output_aliases={2: 0},
        compiler_params=_cparams("arbitrary"), name="moe_gather",
    )(pos, xn, xs0)


def _moe_expert_kernel(te_ref, nt_ref, xs_ref, wg_ref, wu_ref, wd_ref, ys_ref):
    i = pl.program_id(0)

    @pl.when(i < nt_ref[0])
    def _():
        x = xs_ref[...].astype(BF16)
        gate = _dot(x, wg_ref[0])
        up = _dot(x, wu_ref[0])
        act = (gate * jax.nn.sigmoid(gate) * up).astype(BF16)
        ys_ref[...] = _dot(act, wd_ref[0])

    @pl.when(i >= nt_ref[0])
    def _():
        ys_ref[...] = jnp.zeros_like(ys_ref)


def _moe_experts(xs, tile_expert, n_tiles_used, w_gate, w_up, w_down):
    n_rows = xs.shape[0]
    tg = MOE_ROW_TILE
    d_ff = w_gate.shape[2]
    row = pl.BlockSpec((tg, D_MODEL), lambda i, te, nt: (i, 0))
    grid_spec = pltpu.PrefetchScalarGridSpec(
        num_scalar_prefetch=2, grid=(n_rows // tg,),
        in_specs=[row,
                  pl.BlockSpec((1, D_MODEL, d_ff), lambda i, te, nt: (te[i], 0, 0)),
                  pl.BlockSpec((1, D_MODEL, d_ff), lambda i, te, nt: (te[i], 0, 0)),
                  pl.BlockSpec((1, d_ff, D_MODEL), lambda i, te, nt: (te[i], 0, 0))],
        out_specs=row)
    return pl.pallas_call(
        _moe_expert_kernel, grid_spec=grid_spec, out_shape=jax.ShapeDtypeStruct((n_rows, D_MODEL), F32),
        compiler_params=_cparams("arbitrary"), name="moe_experts",
    )(tile_expert, n_tiles_used, xs, w_gate, w_up, w_down)


def _moe_combine_kernel(pos_ref, h_ref, gate_ref, ys_hbm, out_ref, buf, sem):
    tm = h_ref.shape[0]

    def copy(t, k):
        return pltpu.make_async_copy(ys_hbm.at[pl.ds(pos_ref[k, t], 1)], buf.at[k, pl.ds(t, 1)], sem)

    def issue(t, c):
        copy(t, 0).start()
        copy(t, 1).start()
        return c

    def drain(t, c):
        copy(t, 0).wait()
        copy(t, 1).wait()
        return c

    lax.fori_loop(0, tm, issue, 0)
    lax.fori_loop(0, tm, drain, 0)
    gate = gate_ref[...]
    out_ref[...] = h_ref[...] + gate[:, 0:1] * buf[0] + gate[:, 1:2] * buf[1]


def _moe_combine(h2d, gates_t, pos, ys):
    m = h2d.shape[0]
    tm = min(MOE_DMA_CHUNK, m)
    row = pl.BlockSpec((tm, D_MODEL), lambda i: (i, 0))
    return pl.pallas_call(
        _moe_combine_kernel, grid=(m // tm,),
        in_specs=[pl.BlockSpec((2, tm), lambda i: (0, i), memory_space=pltpu.SMEM), row,
                  pl.BlockSpec((tm, 2), lambda i: (i, 0)), pl.BlockSpec(memory_space=pl.ANY)],
        out_specs=row, out_shape=jax.ShapeDtypeStruct((m, D_MODEL), F32),
        scratch_shapes=[pltpu.VMEM((2, tm, D_MODEL), F32), pltpu.SemaphoreType.DMA(())],
        compiler_params=_cparams("arbitrary"), name="moe_combine",
    )(pos, h2d, gates_t, ys)


def _moe_layer(h2d, norm_g, w_router, b_router, w_gate, w_up, w_down):
    m = h2d.shape[0]
    tg = MOE_ROW_TILE
    xn, idx, gate, rank, cnt = _moe_router(h2d, norm_g, w_router, b_router)
    counts = cnt[:, 0]
    tiles = (counts + tg - 1) // tg
    tile_end = jnp.cumsum(tiles)
    base = (tile_end - tiles) * tg
    pos = base[idx] + rank
    n_tiles = (2 * m) // tg + N_EXPERTS
    tile_expert = jnp.minimum(jnp.searchsorted(tile_end, jnp.arange(n_tiles), side="right"),
                              N_EXPERTS - 1).astype(I32)
    xs = _moe_gather(xn, pos, n_tiles * tg)
    ys = _moe_experts(xs, tile_expert, tile_end[-1:].astype(I32), w_gate, w_up, w_down)
    return _moe_combine(h2d, gate.T, pos, ys)


def kernel(x_prompt, x_sample, cache_k, cache_v, cache_kidx, state_s5_re, state_s5_im, page_table,
           attn_norm_g, attn_w_in, attn_g_q, attn_g_k, attn_g_kidx, attn_w_o,
           dense_norm_g, dense_w_gate, dense_w_up, dense_w_down,
           s5_norm_g, s5_w_in, s5_a_re, s5_a_im, s5_log_dt, s5_b_re, s5_b_im, s5_c_re, s5_c_im,
           s5_d, s5_w_glu,
           moe_norm_g, moe_w_router, moe_b_router, moe_w_gate, moe_w_up, moe_w_down):
    nbp, seq, d = x_prompt.shape
    nbs, t_new, _ = x_sample.shape
    mp, ms = nbp * seq, nbs * t_new
    tp = 8
    xp2 = x_prompt.reshape(mp, d)
    xs2 = x_sample.reshape(ms, d)

    w_in_t = attn_w_in[0].T.astype(BF16)
    o1, o2, o3, o4, o5 = np.cumsum([Q_COLS, KV_COLS, KV_COLS, QI_COLS, IDX_DIM]).tolist()
    w_kw = jnp.pad(w_in_t[o4:], ((0, LANES - (w_in_t.shape[0] - o4)), (0, 0)))
    w_parts = (w_in_t[:o1], w_in_t[o1:o3], w_in_t[o3:o4], w_kw)
    ap = (attn_norm_g[0], w_parts, attn_g_k[0], attn_g_kidx[0])

    q_p, qi_p, wi_p, kt_p, vt_p, kit_p, ktb, vtb, kitb = _attn_proj(xp2, *ap, seq_len=seq)
    o_p = _attn_prompt(q_p, qi_p, wi_p, ktb, vtb, kitb, attn_g_q[0])

    q_s, qi_s, wi_s, k_s, v_s, ki_s = _attn_proj(xs2, *ap)
    pad_t = lambda a: jnp.pad(a.reshape(nbs, t_new, a.shape[-1]), ((0, 0), (0, tp - t_new), (0, 0)))
    n_pool = cache_k.shape[1]
    ck = jnp.transpose(cache_k[0], (0, 2, 3, 1)).reshape(n_pool, KV_COLS, PAGE_SIZE)
    cv = jnp.transpose(cache_v[0], (0, 2, 3, 1)).reshape(n_pool, KV_COLS, PAGE_SIZE)
    cki = jnp.transpose(cache_kidx[0], (0, 2, 1))
    o_s = _attn_sample(pad_t(q_s), pad_t(qi_s.astype(F32)), pad_t(wi_s), pad_t(k_s), pad_t(v_s), pad_t(ki_s),
                       attn_g_q[0], page_table, ck, cv, cki, t_new=t_new)
    o_s = o_s[:, :t_new].reshape(ms, Q_COLS)

    fp = (attn_w_o[0].astype(BF16), dense_norm_g[0], dense_w_gate[0].astype(BF16),
          dense_w_up[0].astype(BF16), dense_w_down[0].astype(BF16))
    hp = _attn_out_ffn(xp2, o_p, *fp)
    hs = _attn_out_ffn(xs2, o_s, *fp)

    abr, abi, bbr, bbi = _s5_params(s5_a_re[0], s5_a_im[0], s5_log_dt[0], s5_b_re[0], s5_b_im[0])
    sp = (s5_norm_g[0], s5_w_in[0].astype(BF16), abr, abi, bbr, bbi, s5_c_re[0], s5_c_im[0], s5_d[0])
    zeros_p = jnp.zeros((nbp, SSM_COLS), F32)
    z_p, srp, sip = _s5_core(hp.reshape(nbp, seq, d), zeros_p, zeros_p, *sp, tt=S5_TIME_TILE)
    z_s, srs, sis = _s5_core(hs.reshape(nbs, t_new, d), state_s5_re[0].reshape(nbs, SSM_COLS),
                             state_s5_im[0].reshape(nbs, SSM_COLS), *sp, tt=t_new)
    w_glu = s5_w_glu[0].astype(BF16)
    hp = _glu_residual(hp, z_p, w_glu)
    hs = _glu_residual(hs, z_s, w_glu)

    mw = (moe_norm_g[0], moe_w_router[0], moe_b_router[0], moe_w_gate[0].astype(BF16),
          moe_w_up[0].astype(BF16), moe_w_down[0].astype(BF16))
    hp = _moe_layer(hp, *mw)
    hs = _moe_layer(hs, *mw)

    fm_kv = lambda a: jnp.transpose(a.reshape(nbp, N_KV_HEADS, HEAD_DIM, seq), (0, 3, 1, 2))[None]
    st = lambda a, n: a.reshape(1, n, SSM_GROUPS, SSM_STATE)
    return (hp.reshape(nbp, seq, d), hs.reshape(nbs, t_new, d),
            fm_kv(kt_p), fm_kv(vt_p), jnp.transpose(kit_p, (0, 2, 1))[None],
            st(srp, nbp), st(sip, nbp),
            k_s.reshape(1, nbs, t_new, N_KV_HEADS, HEAD_DIM), v_s.reshape(1, nbs, t_new, N_KV_HEADS, HEAD_DIM),
            ki_s.reshape(1, nbs, t_new, IDX_DIM),
            st(srs, nbs), st(sis, nbs))
```

```python
import functools
import math

import jax
import jax.numpy as jnp
import numpy as np
from jax import lax
from jax.experimental import pallas as pl
from jax.experimental.pallas import tpu as pltpu

F32 = jnp.float32
BF16 = jnp.bfloat16
I32 = jnp.int32

D_MODEL = 1024
N_HEADS = 16
N_KV_HEADS = 4
HEAD_DIM = 64
Q_PER_KV = N_HEADS // N_KV_HEADS
IDX_HEADS = 16
IDX_DIM = 64
TOPK_MAX = 256
PAGE_SIZE = 128
ATTN_SCALE = HEAD_DIM ** -0.5
INDEX_SCALE = (IDX_HEADS ** -0.5) * (IDX_DIM ** -0.5)
Q_COLS = N_HEADS * HEAD_DIM
KV_COLS = N_KV_HEADS * HEAD_DIM
QI_COLS = IDX_HEADS * IDX_DIM
SSM_GROUP = 16
SSM_GROUPS = D_MODEL // SSM_GROUP
SSM_STATE = 64
SSM_COLS = SSM_GROUPS * SSM_STATE
N_EXPERTS = 8
RMS_EPS = 1e-6

LANES = 128
MXU_DIM = 256
VMEM_LIMIT = 56 * 1024 * 1024
INT_MIN = -(2 ** 31)
NEG_INF_KEY = INT_MIN + 0x7FFFFF

ROW_TILE = 512
ATTN_Q_TILE = 128
ATTN_KEY_BUCKET = 512
S5_TIME_TILE = 32
S5_SCAN_ELEMS = 8192
MOE_ROW_TILE = 256
MOE_DMA_CHUNK = 256


def _cparams(*sem):
    return pltpu.CompilerParams(dimension_semantics=sem, vmem_limit_bytes=VMEM_LIMIT)


def _rms(x, g):
    return x * lax.rsqrt(jnp.mean(x * x, axis=-1, keepdims=True) + RMS_EPS) * g


def _dot(a, b):
    return jnp.dot(a, b, preferred_element_type=F32)


def _dot_nt(a, b):
    return lax.dot_general(a, b, (((1,), (1,)), ((), ())), preferred_element_type=F32)


def _full(shape):
    n = len(shape)
    return pl.BlockSpec(shape, lambda *_: (0,) * n)


def _attn_proj_kernel(x_ref, ng_ref, wq_ref, wkv_ref, wqi_ref, wkw_ref, gk_ref, gki_ref,
                      q_ref, qi_ref, wi_ref, k_ref, v_ref, ki_ref, *rest, transposed):
    xn = _rms(x_ref[...], ng_ref[...]).astype(BF16)
    q_ref[...] = _dot_nt(xn, wq_ref[...])
    qi_ref[...] = _dot_nt(xn, wqi_ref[...]).astype(BF16)
    if transposed:
        kb_ref, vb_ref, kib_ref = rest
        tm = xn.shape[0]
        kv = _dot_nt(wkv_ref[...], xn)
        k = kv[:KV_COLS].reshape(N_KV_HEADS, HEAD_DIM, tm)
        k = k * lax.rsqrt(jnp.mean(k * k, axis=1, keepdims=True) + RMS_EPS) * gk_ref[...][None]
        k = k.reshape(KV_COLS, tm)
        v = kv[KV_COLS:]
        kw = _dot_nt(wkw_ref[...], xn)
        ki = kw[:IDX_DIM]
        ki = ki * lax.rsqrt(jnp.mean(ki * ki, axis=0, keepdims=True) + RMS_EPS) * gki_ref[...]
        wi_ref[...] = kw[IDX_DIM:IDX_DIM + IDX_HEADS].T
        k_ref[0] = k
        v_ref[0] = v
        ki_ref[0] = ki
        kb_ref[0] = k.astype(BF16)
        vb_ref[0] = v.astype(BF16)
        kib_ref[0] = ki.astype(BF16)
    else:
        kv = _dot_nt(xn, wkv_ref[...])
        gk = gk_ref[...]
        ks = []
        for h in range(N_KV_HEADS):
            seg = kv[:, h * HEAD_DIM:(h + 1) * HEAD_DIM]
            ks.append(_rms(seg, gk))
        k_ref[...] = jnp.concatenate(ks, axis=-1)
        v_ref[...] = kv[:, KV_COLS:]
        kw = _dot_nt(xn, wkw_ref[...])
        ki_ref[...] = _rms(kw[:, :IDX_DIM], gki_ref[...])
        wi_ref[...] = kw[:, IDX_DIM:IDX_DIM + IDX_HEADS]


def _attn_proj(x2d, norm_g, w_parts, g_k, g_kidx, *, seq_len=None):
    m = x2d.shape[0]
    tm = min(ROW_TILE, m)
    wq, wkv, wqi, wkw = w_parts
    transposed = seq_len is not None
    row = lambda cols: pl.BlockSpec((tm, cols), lambda i: (i, 0))
    in_specs = [row(D_MODEL), _full((1, D_MODEL)), _full(wq.shape), _full(wkv.shape), _full(wqi.shape),
                _full(wkw.shape)]
    out_shape = [jax.ShapeDtypeStruct((m, Q_COLS), F32), jax.ShapeDtypeStruct((m, QI_COLS), BF16),
                 jax.ShapeDtypeStruct((m, IDX_HEADS), F32)]
    out_specs = [row(Q_COLS), row(QI_COLS), row(IDX_HEADS)]
    if transposed:
        nb, per = m // seq_len, seq_len // tm
        fm = lambda rows: pl.BlockSpec((1, rows, tm), lambda i: (i // per, 0, i % per))
        in_specs += [_full((HEAD_DIM, 1)), _full((IDX_DIM, 1))]
        gk_in, gki_in = g_k.reshape(HEAD_DIM, 1), g_kidx.reshape(IDX_DIM, 1)
        for dt in (F32, BF16):
            out_shape += [jax.ShapeDtypeStruct((nb, KV_COLS, seq_len), dt),
                          jax.ShapeDtypeStruct((nb, KV_COLS, seq_len), dt),
                          jax.ShapeDtypeStruct((nb, IDX_DIM, seq_len), dt)]
            out_specs += [fm(KV_COLS), fm(KV_COLS), fm(IDX_DIM)]
    else:
        in_specs += [_full((1, HEAD_DIM)), _full((1, IDX_DIM))]
        gk_in, gki_in = g_k.reshape(1, HEAD_DIM), g_kidx.reshape(1, IDX_DIM)
        out_shape += [jax.ShapeDtypeStruct((m, KV_COLS), F32), jax.ShapeDtypeStruct((m, KV_COLS), F32),
                      jax.ShapeDtypeStruct((m, IDX_DIM), F32)]
        out_specs += [row(KV_COLS), row(KV_COLS), row(IDX_DIM)]
    return pl.pallas_call(
        functools.partial(_attn_proj_kernel, transposed=transposed),
        grid=(m // tm,), in_specs=in_specs, out_specs=out_specs, out_shape=out_shape,
        compiler_params=_cparams("parallel"), name="attn_proj",
    )(x2d, norm_g.reshape(1, D_MODEL), wq, wkv, wqi, wkw, gk_in, gki_in)


def _sortable_key(score):
    score = jnp.where(score == 0.0, 0.0, score)
    bits = pltpu.bitcast(score, I32)
    return bits ^ (lax.shift_right_arithmetic(bits, 31) & 0x7FFFFFFF)


def _count(mask):
    return jnp.sum(jnp.where(mask, 1.0, 0.0), axis=-1, keepdims=True)


def _topk_select(key, pos, k, lim_ref, pos_bits, unroll=False):
    r = key.shape[0]
    kf = float(k)
    thr0 = jnp.where(_count(key >= 0) >= kf, 0, INT_MIN).astype(I32)

    def value_bit(i, thr):
        cand = thr | lax.shift_left(jnp.int32(1), 30 - i)
        return jnp.where(_count(key >= cand) >= kf, cand, thr)

    thr = lax.fori_loop(0, 31, value_bit, thr0, unroll=unroll)
    above = key > thr
    tied = key == thr
    take = kf - _count(above)
    lim_ref[...] = jnp.full((r, 1), 2 ** pos_bits, I32)
    excess = (_count(tied) > take) & (thr > NEG_INF_KEY)

    @pl.when(jnp.max(jnp.where(excess, 1.0, 0.0)) > 0.0)
    def _():
        def pos_bit(i, lo):
            cand = lo | lax.shift_left(jnp.int32(1), pos_bits - 1 - i)
            return jnp.where(_count(tied & (pos < cand)) < take, cand, lo)

        lim_ref[...] = lax.fori_loop(0, pos_bits, pos_bit, jnp.zeros((r, 1), I32))

    return above | (tied & (pos <= lim_ref[...]))


def _norm_q_heads(q, gq):
    scale = ATTN_SCALE * math.log2(math.e)
    return [_rms(q[:, h * HEAD_DIM:(h + 1) * HEAD_DIM], gq) * scale for h in range(N_HEADS)]


def _attn_prompt_kernel(q_ref, qi_ref, wi_ref, kt_ref, vt_ref, kit_ref, gq_ref, o_ref, lim_ref, *, tq, ksel):
    j = pl.program_id(1)
    seq = kt_ref.shape[2]
    n_buckets = seq // ATTN_KEY_BUCKET
    per_bucket = ATTN_KEY_BUCKET // tq

    def block(s):
        q_pos = j * tq + lax.broadcasted_iota(I32, (tq, 1), 0)
        key_pos = lax.broadcasted_iota(I32, (1, s), 1)
        causal = key_pos <= q_pos
        qi = qi_ref[...]
        wi = wi_ref[...]
        qi_rows = jnp.concatenate([qi[:, h * IDX_DIM:(h + 1) * IDX_DIM] for h in range(IDX_HEADS)], axis=0)
        tiles = []
        for c0 in range(0, s, MXU_DIM):
            sh = _dot(qi_rows, kit_ref[0, :, c0:c0 + MXU_DIM])
            acc = wi[:, 0:1] * jnp.maximum(sh[:tq], 0.0)
            for h in range(1, IDX_HEADS):
                acc = acc + wi[:, h:h + 1] * jnp.maximum(sh[h * tq:(h + 1) * tq], 0.0)
            tiles.append(acc)
        score = jnp.where(causal, jnp.concatenate(tiles, axis=1) * INDEX_SCALE, -jnp.inf)

        heads = _norm_q_heads(q_ref[...], gq_ref[...])
        raw = []
        for n in range(N_KV_HEADS):
            qn = jnp.concatenate(heads[n * Q_PER_KV:(n + 1) * Q_PER_KV], axis=0).astype(BF16)
            raw.append(_dot(qn, kt_ref[0, n * HEAD_DIM:(n + 1) * HEAD_DIM, :s]))

        sel = _topk_select(_sortable_key(score), key_pos, ksel, lim_ref, pos_bits=int(math.log2(seq)),
                           unroll=True)
        bias = jnp.where(sel & causal, 0.0, -jnp.inf)[None]

        outs = [None] * N_HEADS
        for n in range(N_KV_HEADS):
            vn = vt_ref[0, n * HEAD_DIM:(n + 1) * HEAD_DIM, :s]
            logits = raw[n].reshape(Q_PER_KV, tq, s) + bias
            m = jnp.max(logits, axis=-1, keepdims=True)
            p = jnp.exp2(logits - m)
            l = jnp.sum(p, axis=-1, keepdims=True)
            pv = _dot_nt(p.reshape(Q_PER_KV * tq, s).astype(BF16), vn)
            pv = pv / l.reshape(Q_PER_KV * tq, 1)
            for g in range(Q_PER_KV):
                outs[n * Q_PER_KV + g] = pv[g * tq:(g + 1) * tq]
        o_ref[...] = jnp.concatenate(outs, axis=-1).astype(BF16)

    for e in range(n_buckets):
        pl.when(j // per_bucket == e)(functools.partial(block, (e + 1) * ATTN_KEY_BUCKET))


def _attn_prompt(q, qi, wi, ktb, vtb, kitb, g_q):
    nb, _, seq = ktb.shape
    tq = ATTN_Q_TILE
    nq = seq // tq
    ksel = min(TOPK_MAX, seq // 4)
    row = lambda cols: pl.BlockSpec((tq, cols), lambda b, j: (b * nq + j, 0))
    fm = lambda rows: pl.BlockSpec((1, rows, seq), lambda b, j: (b, 0, 0))
    return pl.pallas_call(
        functools.partial(_attn_prompt_kernel, tq=tq, ksel=ksel),
        grid=(nb, nq),
        in_specs=[row(Q_COLS), row(QI_COLS), row(IDX_HEADS), fm(KV_COLS), fm(KV_COLS), fm(IDX_DIM),
                  _full((1, HEAD_DIM))],
        out_specs=row(Q_COLS),
        out_shape=jax.ShapeDtypeStruct((nb * seq, Q_COLS), BF16),
        scratch_shapes=[pltpu.VMEM((tq, 1), I32)],
        compiler_params=_cparams("parallel", "arbitrary"), name="attn_prompt",
    )(q, qi, wi, ktb, vtb, kitb, g_q.reshape(1, HEAD_DIM))


def _attn_sample_kernel(pt_ref, q_ref, qi_ref, wi_ref, kn_ref, vn_ref, kin_ref, gq_ref,
                        ck_hbm, cv_hbm, cki_hbm, o_ref, kbuf, vbuf, kibuf, sems, lim_ref,
                        *, n_pages, t_new, ksel):
    b = pl.program_id(0)
    nb = pl.num_programs(0)
    past = n_pages * PAGE_SIZE
    tp = q_ref.shape[1]

    def page_copies(batch, slot, p):
        page = pt_ref[batch, p]
        cols = pl.ds(p * PAGE_SIZE, PAGE_SIZE)
        return (pltpu.make_async_copy(ck_hbm.at[page], kbuf.at[slot, :, cols], sems.at[0, slot]),
                pltpu.make_async_copy(cv_hbm.at[page], vbuf.at[slot, :, cols], sems.at[1, slot]),
                pltpu.make_async_copy(cki_hbm.at[page], kibuf.at[slot, :, cols], sems.at[2, slot]))

    def fetch(batch, slot):
        def issue(p, c):
            for cp in page_copies(batch, slot, p):
                cp.start()
            return c
        lax.fori_loop(0, n_pages, issue, 0)

    def wait(batch, slot):
        def done(p, c):
            for cp in page_copies(batch, slot, p):
                cp.wait()
            return c
        lax.fori_loop(0, n_pages, done, 0)

    slot = b % 2

    @pl.when(b == 0)
    def _():
        fetch(0, 0)

    @pl.when(b + 1 < nb)
    def _():
        fetch(b + 1, 1 - slot)

    wait(b, slot)

    nrow = N_HEADS * tp
    s_all = past + LANES

    def new_rows(ref):
        x = ref[0]
        return jnp.concatenate([x, jnp.zeros((LANES - tp, x.shape[1]), F32)], axis=0).astype(BF16)

    qi = qi_ref[0]
    wi = wi_ref[0]
    qi_rows = jnp.concatenate([qi[:, h * IDX_DIM:(h + 1) * IDX_DIM] for h in range(IDX_HEADS)],
                              axis=0).astype(BF16)
    wi_rows = jnp.concatenate([wi[:, h:h + 1] for h in range(IDX_HEADS)], axis=0)
    s_idx = jnp.concatenate([_dot(qi_rows, kibuf[slot].astype(BF16)),
                             _dot_nt(qi_rows, new_rows(kin_ref))], axis=1)
    s_idx = wi_rows * jnp.maximum(s_idx, 0.0)
    score = jnp.sum(s_idx.reshape(IDX_HEADS, tp, s_all), axis=0) * INDEX_SCALE
    q_pos = past + lax.broadcasted_iota(I32, (tp, 1), 0)
    key_pos = lax.broadcasted_iota(I32, (1, s_all), 1)
    causal = (key_pos <= q_pos) & (key_pos < past + t_new)
    score = jnp.where(causal, score, -jnp.inf)

    heads = _norm_q_heads(q_ref[0], gq_ref[...])
    zero = jnp.zeros((Q_PER_KV * tp, HEAD_DIM), F32)
    qbd = jnp.concatenate([
        jnp.concatenate([jnp.concatenate(heads[n * Q_PER_KV:(n + 1) * Q_PER_KV], axis=0) if c == n else zero
                         for c in range(N_KV_HEADS)], axis=1)
        for n in range(N_KV_HEADS)], axis=0).astype(BF16)
    logits = jnp.concatenate([_dot(qbd, kbuf[slot].astype(BF16)),
                              _dot_nt(qbd, new_rows(kn_ref))], axis=1)

    sel = _topk_select(_sortable_key(score), key_pos, ksel, lim_ref, pos_bits=int(math.ceil(math.log2(s_all))),
                       unroll=True)
    bias = jnp.where(sel & causal, 0.0, -jnp.inf)[None]
    logits = (logits.reshape(N_HEADS, tp, s_all) + bias).reshape(nrow, s_all)
    m = jnp.max(logits, axis=-1, keepdims=True)
    p = jnp.exp2(logits - m)
    l = jnp.sum(p, axis=-1, keepdims=True)
    pb = p.astype(BF16)
    pv = _dot_nt(pb[:, :past], vbuf[slot].astype(BF16)) + _dot(pb[:, past:], new_rows(vn_ref))
    pv = pv / l
    outs = []
    for n in range(N_KV_HEADS):
        for g in range(Q_PER_KV):
            r0 = (n * Q_PER_KV + g) * tp
            outs.append(pv[r0:r0 + tp, n * HEAD_DIM:(n + 1) * HEAD_DIM])
    o_ref[0] = jnp.concatenate(outs, axis=-1).astype(BF16)


def _attn_sample(q, qi, wi, k_new, v_new, ki_new, g_q, page_table, ck, cv, cki, *, t_new):
    nb, tp, _ = q.shape
    n_pages = page_table.shape[1]
    past = n_pages * PAGE_SIZE
    ksel = min(TOPK_MAX, (past + t_new) // 4)
    blk = lambda cols: pl.BlockSpec((1, tp, cols), lambda b, pt: (b, 0, 0))
    grid_spec = pltpu.PrefetchScalarGridSpec(
        num_scalar_prefetch=1, grid=(nb,),
        in_specs=[blk(Q_COLS), blk(QI_COLS), blk(IDX_HEADS), blk(KV_COLS), blk(KV_COLS), blk(IDX_DIM),
                  pl.BlockSpec((1, HEAD_DIM), lambda b, pt: (0, 0)),
                  pl.BlockSpec(memory_space=pl.ANY), pl.BlockSpec(memory_space=pl.ANY),
                  pl.BlockSpec(memory_space=pl.ANY)],
        out_specs=blk(Q_COLS),
        scratch_shapes=[pltpu.VMEM((2, KV_COLS, past), F32), pltpu.VMEM((2, KV_COLS, past), F32),
                        pltpu.VMEM((2, IDX_DIM, past), F32), pltpu.SemaphoreType.DMA((3, 2)),
                        pltpu.VMEM((tp, 1), I32)])
    return pl.pallas_call(
        functools.partial(_attn_sample_kernel, n_pages=n_pages, t_new=t_new, ksel=ksel),
        grid_spec=grid_spec, out_shape=jax.ShapeDtypeStruct((nb, tp, Q_COLS), BF16),
        compiler_params=_cparams("arbitrary"), name="attn_sample",
    )(page_table, q, qi, wi, k_new, v_new, ki_new, g_q.reshape(1, HEAD_DIM), ck, cv, cki)


def _attn_out_ffn_kernel(x_ref, o_ref, wo_ref, ng_ref, wg_ref, wu_ref, wd_ref, out_ref, h_ref, xn_ref, acc_ref):
    j = pl.program_id(1)

    @pl.when(j == 0)
    def _():
        h = x_ref[...] + _dot(o_ref[...], wo_ref[...])
        h_ref[...] = h
        xn_ref[...] = _rms(h, ng_ref[...]).astype(BF16)
        acc_ref[...] = jnp.zeros_like(acc_ref)

    xn = xn_ref[...]
    gate = _dot(xn, wg_ref[...])
    up = _dot(xn, wu_ref[...])
    act = (gate * jax.nn.sigmoid(gate) * up).astype(BF16)
    acc_ref[...] += _dot(act, wd_ref[...])

    @pl.when(j == pl.num_programs(1) - 1)
    def _():
        out_ref[...] = h_ref[...] + acc_ref[...]


def _attn_out_ffn(x2d, o, w_o, norm_g, w_gate, w_up, w_down):
    m = x2d.shape[0]
    tm = min(ROW_TILE, m)
    d_ff = w_gate.shape[1]
    tf = d_ff // 2
    row = lambda cols: pl.BlockSpec((tm, cols), lambda i, j: (i, 0))
    return pl.pallas_call(
        _attn_out_ffn_kernel, grid=(m // tm, d_ff // tf),
        in_specs=[row(D_MODEL), row(Q_COLS), pl.BlockSpec(w_o.shape, lambda i, j: (0, 0)),
                  pl.BlockSpec((1, D_MODEL), lambda i, j: (0, 0)),
                  pl.BlockSpec((D_MODEL, tf), lambda i, j: (0, j)),
                  pl.BlockSpec((D_MODEL, tf), lambda i, j: (0, j)),
                  pl.BlockSpec((tf, D_MODEL), lambda i, j: (j, 0))],
        out_specs=row(D_MODEL), out_shape=jax.ShapeDtypeStruct((m, D_MODEL), F32),
        scratch_shapes=[pltpu.VMEM((tm, D_MODEL), F32), pltpu.VMEM((tm, D_MODEL), BF16),
                        pltpu.VMEM((tm, D_MODEL), F32)],
        compiler_params=_cparams("parallel", "arbitrary"), name="attn_out_ffn",
    )(x2d, o, w_o, norm_g.reshape(1, D_MODEL), w_gate, w_up, w_down)


def _s5_params_kernel(are_ref, aim_ref, ldt_ref, bre_ref, bim_ref, abr_ref, abi_ref, bbr_ref, bbi_ref):
    dt = jnp.exp(ldt_ref[...])
    lr = are_ref[...] * dt
    li = aim_ref[...] * dt
    mag = jnp.exp(lr)
    abr = mag * jnp.cos(li)
    abi = mag * jnp.sin(li)
    den = lr * lr + li * li
    fr = ((abr - 1.0) * lr + abi * li) / den * dt
    fi = (abi * lr - (abr - 1.0) * li) / den * dt
    abr_ref[...] = abr
    abi_ref[...] = abi
    bre = bre_ref[...]
    bim = bim_ref[...]
    bbr_ref[...] = fr * bre - fi * bim
    bbi_ref[...] = fr * bim + fi * bre


def _s5_params(a_re, a_im, log_dt, b_re, b_im):
    g, p, c = b_re.shape
    bt = lambda b: jnp.transpose(b, (0, 2, 1))
    shp3 = jax.ShapeDtypeStruct((g, 1, p), F32)
    shpb = jax.ShapeDtypeStruct((g, c, p), F32)
    return pl.pallas_call(
        _s5_params_kernel, out_shape=[shp3, shp3, shpb, shpb], name="s5_params",
    )(a_re.reshape(g, 1, p), a_im.reshape(g, 1, p), log_dt.reshape(g, 1, 1), bt(b_re), bt(b_im))


def _block_diag_tiles(w, groups_per_tile):
    g, a, b = w.shape
    eye = jnp.eye(groups_per_tile, dtype=w.dtype)
    w = w.reshape(g // groups_per_tile, groups_per_tile, a, b)
    out = w[:, :, :, None, :] * eye[None, :, None, :, None]
    return out.reshape(g // groups_per_tile, groups_per_tile * a, groups_per_tile * b)


def _s5_core_kernel(x_ref, ng_ref, win_ref, perm_ref, permt_ref, bbr_ref, bbi_ref, cr_ref, ci_ref,
                    abr_ref, abi_ref, d_ref, s0r_ref, s0i_ref, z_ref, sr_ref, si_ref, xr_ref, xi_ref,
                    *, nb, tt):
    i = pl.program_id(0)
    rows = nb * tt
    gpt = MXU_DIM // SSM_GROUP
    n_kt = D_MODEL // MXU_DIM
    st_cols = gpt * SSM_STATE

    @pl.when(i == 0)
    def _():
        sr_ref[...] = s0r_ref[...]
        si_ref[...] = s0i_ref[...]

    x = x_ref[...].reshape(rows, D_MODEL)
    xn = _rms(x, ng_ref[...]).astype(BF16)
    xn = _dot(perm_ref[...], xn).astype(BF16)
    u = _dot(xn, win_ref[...])
    ub = u.astype(BF16)
    for kt in range(n_kt):
        uk = ub[:, kt * MXU_DIM:(kt + 1) * MXU_DIM]
        xr_ref[:, kt * st_cols:(kt + 1) * st_cols] = _dot(uk, bbr_ref[kt])
        xi_ref[:, kt * st_cols:(kt + 1) * st_cols] = _dot(uk, bbi_ref[kt])

    cw = S5_SCAN_ELEMS // nb
    for c0 in range(0, SSM_COLS, cw):
        cols = pl.ds(c0, cw)
        ar = jnp.broadcast_to(abr_ref[:, cols], (nb, cw))
        ai = jnp.broadcast_to(abi_ref[:, cols], (nb, cw))

        def step(t, carry):
            hr, hi = carry
            r = pl.ds(pl.multiple_of(t * nb, nb), nb)
            nr = ar * hr - ai * hi + xr_ref[r, cols]
            ni = ar * hi + ai * hr + xi_ref[r, cols]
            xr_ref[r, cols] = nr
            xi_ref[r, cols] = ni
            return nr, ni

        hr, hi = lax.fori_loop(0, tt, step, (sr_ref[:, cols], si_ref[:, cols]))
        sr_ref[:, cols] = hr
        si_ref[:, cols] = hi

    ys = []
    for kt in range(n_kt):
        sc = pl.ds(kt * st_cols, st_cols)
        ys.append(_dot(xr_ref[:, sc].astype(BF16), cr_ref[kt]) - _dot(xi_ref[:, sc].astype(BF16), ci_ref[kt]))
    y = jnp.concatenate(ys, axis=-1) + d_ref[...] * u
    z = jax.nn.gelu(y).astype(BF16)
    z_ref[...] = _dot(permt_ref[...], z).astype(BF16).reshape(z_ref.shape)


def _s5_core(x, s0_re, s0_im, norm_g, w_in, abr, abi, bbr, bbi, c_re, c_im, d_skip, *, tt):
    nb, t, _ = x.shape
    rows = nb * tt
    gpt = MXU_DIM // SSM_GROUP
    r = np.arange(rows)
    perm_np = np.zeros((rows, rows), np.float32)
    perm_np[r, (r % nb) * tt + r // nb] = 1.0
    perm = jnp.asarray(perm_np, BF16)
    permt = jnp.asarray(perm_np.T, BF16)
    bbr_t = _block_diag_tiles(bbr, gpt).astype(BF16)
    bbi_t = _block_diag_tiles(bbi, gpt).astype(BF16)
    ct = lambda c: _block_diag_tiles(jnp.transpose(c, (0, 2, 1)), gpt).astype(BF16)
    if tt % 8 == 0:
        x_in, x_spec = x, pl.BlockSpec((nb, tt, D_MODEL), lambda i: (0, i, 0))
    else:
        assert tt == t
        x_in, x_spec = x.reshape(rows, D_MODEL), pl.BlockSpec((rows, D_MODEL), lambda i: (0, 0))
    z_spec = (pl.BlockSpec((nb, tt, D_MODEL), lambda i: (0, i, 0)) if tt % 8 == 0
              else pl.BlockSpec((rows, D_MODEL), lambda i: (0, 0)))
    z_shape = (nb, t, D_MODEL) if tt % 8 == 0 else (rows, D_MODEL)
    st = jax.ShapeDtypeStruct((nb, SSM_COLS), F32)
    z, sr, si = pl.pallas_call(
        functools.partial(_s5_core_kernel, nb=nb, tt=tt), grid=(t // tt,),
        in_specs=[x_spec, _full((1, D_MODEL)), _full(w_in.shape), _full(perm.shape), _full(permt.shape),
                  _full(bbr_t.shape), _full(bbi_t.shape), _full((D_MODEL // MXU_DIM, gpt * SSM_STATE, MXU_DIM)),
                  _full((D_MODEL // MXU_DIM, gpt * SSM_STATE, MXU_DIM)),
                  _full((1, SSM_COLS)), _full((1, SSM_COLS)), _full((1, D_MODEL)),
                  _full((nb, SSM_COLS)), _full((nb, SSM_COLS))],
        out_specs=[z_spec, _full((nb, SSM_COLS)), _full((nb, SSM_COLS))],
        out_shape=[jax.ShapeDtypeStruct(z_shape, BF16), st, st],
        scratch_shapes=[pltpu.VMEM((rows, SSM_COLS), F32), pltpu.VMEM((rows, SSM_COLS), F32)],
        compiler_params=_cparams("arbitrary"), name="s5_core",
    )(x_in, norm_g.reshape(1, D_MODEL), w_in, perm, permt, bbr_t, bbi_t, ct(c_re), ct(c_im),
      abr.reshape(1, SSM_COLS), abi.reshape(1, SSM_COLS), d_skip.reshape(1, D_MODEL), s0_re, s0_im)
    return z.reshape(nb * t, D_MODEL), sr, si


def _glu_kernel(h_ref, z_ref, w_ref, out_ref):
    gl = _dot(z_ref[...], w_ref[...])
    out_ref[...] = h_ref[...] + gl[:, :D_MODEL] * jax.nn.sigmoid(gl[:, D_MODEL:])


def _glu_residual(h2d, z, w_glu):
    m = h2d.shape[0]
    tm = min(ROW_TILE, m)
    row = pl.BlockSpec((tm, D_MODEL), lambda i: (i, 0))
    return pl.pallas_call(
        _glu_kernel, grid=(m // tm,), in_specs=[row, row, _full(w_glu.shape)], out_specs=row,
        out_shape=jax.ShapeDtypeStruct((m, D_MODEL), F32), compiler_params=_cparams("parallel"), name="s5_glu",
    )(h2d, z, w_glu)


def _moe_router_kernel(h_ref, ng_ref, wr_ref, br_ref, xn_ref, idx_ref, gate_ref, rank_ref, cnt_ref, carry_ref):
    i = pl.program_id(0)
    tm = h_ref.shape[0]

    @pl.when(i == 0)
    def _():
        carry_ref[...] = jnp.zeros_like(carry_ref)

    xn = _rms(h_ref[...], ng_ref[...])
    xn_ref[...] = xn
    logits = lax.dot_general(wr_ref[...], xn, (((1,), (1,)), ((), ())), preferred_element_type=F32,
                             precision=lax.Precision.HIGHEST) + br_ref[...]
    eid = lax.broadcasted_iota(I32, (N_EXPERTS, tm), 0).astype(F32)
    m1 = jnp.max(logits, axis=0, keepdims=True)
    i1 = jnp.min(jnp.where(logits == m1, eid, float(N_EXPERTS)), axis=0, keepdims=True)
    rest = jnp.where(eid == i1, -jnp.inf, logits)
    m2 = jnp.max(rest, axis=0, keepdims=True)
    i2 = jnp.min(jnp.where(rest == m2, eid, float(N_EXPERTS)), axis=0, keepdims=True)
    e2 = jnp.exp(m2 - m1)
    den = 1.0 + e2
    idx_ref[...] = jnp.concatenate([i1, i2], axis=0).astype(I32)
    gate_ref[...] = jnp.concatenate([1.0 / den, e2 / den], axis=0)

    onehot = jnp.where((eid == i1) | (eid == i2), 1.0, 0.0)
    src = lax.broadcasted_iota(I32, (tm, tm), 0)
    dst = lax.broadcasted_iota(I32, (tm, tm), 1)
    tri = jnp.where(src <= dst, 1.0, 0.0).astype(BF16)
    cum = _dot(onehot.astype(BF16), tri) + carry_ref[:, 0:1]
    before = cum - 1.0
    r1 = jnp.sum(jnp.where(eid == i1, before, 0.0), axis=0, keepdims=True)
    r2 = jnp.sum(jnp.where(eid == i2, before, 0.0), axis=0, keepdims=True)
    rank_ref[...] = jnp.concatenate([r1, r2], axis=0).astype(I32)
    total = cum[:, tm - 1:tm]
    carry_ref[...] = jnp.broadcast_to(total, carry_ref.shape)
    cnt_ref[...] = jnp.broadcast_to(total, cnt_ref.shape).astype(I32)


def _moe_router(h2d, norm_g, w_router, b_router):
    m = h2d.shape[0]
    tm = min(ROW_TILE, m)
    row = pl.BlockSpec((tm, D_MODEL), lambda i: (i, 0))
    col2 = pl.BlockSpec((2, tm), lambda i: (0, i))
    return pl.pallas_call(
        _moe_router_kernel, grid=(m // tm,),
        in_specs=[row, _full((1, D_MODEL)), _full((N_EXPERTS, D_MODEL)), _full((N_EXPERTS, 1))],
        out_specs=[row, col2, col2, col2, _full((N_EXPERTS, LANES))],
        out_shape=[jax.ShapeDtypeStruct((m, D_MODEL), F32), jax.ShapeDtypeStruct((2, m), I32),
                   jax.ShapeDtypeStruct((2, m), F32), jax.ShapeDtypeStruct((2, m), I32),
                   jax.ShapeDtypeStruct((N_EXPERTS, LANES), I32)],
        scratch_shapes=[pltpu.VMEM((N_EXPERTS, LANES), F32)],
        compiler_params=_cparams("arbitrary"), name="moe_router",
    )(h2d, norm_g.reshape(1, D_MODEL), w_router.T, b_router.reshape(N_EXPERTS, 1))


def _moe_scatter_kernel(tend_ref, pos_ref, xn_ref, xs_hbm, zbuf, sem, zsem, *, n_tiles):
    i = pl.program_id(0)
    ch = pos_ref.shape[1]
    tg = zbuf.shape[0]

    @pl.when(i == 0)
    def _():
        zbuf[...] = jnp.zeros_like(zbuf)
        used = tend_ref[N_EXPERTS - 1]

        def clears():
            for e in range(N_EXPERTS):
                first = tend_ref[e - 1] if e else 0
                yield tend_ref[e] > first, tend_ref[e] - 1
            for r in range(N_EXPERTS):
                yield used + r < n_tiles, used + r

        for go, tile in clears():
            @pl.when(go)
            def _():
                pltpu.make_async_copy(zbuf, xs_hbm.at[pl.ds(tile * tg, tg)], zsem).start()
        for go, tile in clears():
            @pl.when(go)
            def _():
                pltpu.make_async_copy(zbuf, xs_hbm.at[pl.ds(tile * tg, tg)], zsem).wait()

    def copy(t, k):
        return pltpu.make_async_copy(xn_ref.at[pl.ds(t, 1)], xs_hbm.at[pl.ds(pos_ref[k, t], 1)], sem)

    def issue(t, c):
        copy(t, 0).start()
        copy(t, 1).start()
        return c

    def drain(t, c):
        copy(t, 0).wait()
        copy(t, 1).wait()
        return c

    lax.fori_loop(0, ch, issue, 0, unroll=8)
    lax.fori_loop(0, ch, drain, 0, unroll=8)


def _moe_scatter(xn, pos, tile_end, n_tiles):
    m = xn.shape[0]
    ch = min(MOE_DMA_CHUNK, m)
    tg = MOE_ROW_TILE
    grid_spec = pltpu.PrefetchScalarGridSpec(
        num_scalar_prefetch=1, grid=(m // ch,),
        in_specs=[pl.BlockSpec((2, ch), lambda i, te: (0, i), memory_space=pltpu.SMEM),
                  pl.BlockSpec((ch, D_MODEL), lambda i, te: (i, 0))],
        out_specs=pl.BlockSpec(memory_space=pl.ANY),
        scratch_shapes=[pltpu.VMEM((tg, D_MODEL), F32), pltpu.SemaphoreType.DMA(()),
                        pltpu.SemaphoreType.DMA(())])
    return pl.pallas_call(
        functools.partial(_moe_scatter_kernel, n_tiles=n_tiles), grid_spec=grid_spec,
        out_shape=jax.ShapeDtypeStruct((n_tiles * tg, D_MODEL), F32),
        compiler_params=_cparams("arbitrary"), name="moe_scatter",
    )(tile_end, pos, xn)


def _moe_expert_kernel(te_ref, nt_ref, xs_ref, wg_ref, wu_ref, wd_ref, ys_ref):
    i = pl.program_id(0)

    @pl.when(i < nt_ref[0])
    def _():
        x = xs_ref[...].astype(BF16)
        gate = _dot(x, wg_ref[0])
        up = _dot(x, wu_ref[0])
        act = (gate * jax.nn.sigmoid(gate) * up).astype(BF16)
        ys_ref[...] = _dot(act, wd_ref[0])

    @pl.when(i >= nt_ref[0])
    def _():
        ys_ref[...] = jnp.zeros_like(ys_ref)


def _moe_experts(xs, tile_expert, n_tiles_used, w_gate, w_up, w_down):
    n_rows = xs.shape[0]
    tg = MOE_ROW_TILE
    d_ff = w_gate.shape[2]
    row = pl.BlockSpec((tg, D_MODEL), lambda i, te, nt: (i, 0))
    grid_spec = pltpu.PrefetchScalarGridSpec(
        num_scalar_prefetch=2, grid=(n_rows // tg,),
        in_specs=[row,
                  pl.BlockSpec((1, D_MODEL, d_ff), lambda i, te, nt: (te[i], 0, 0)),
                  pl.BlockSpec((1, D_MODEL, d_ff), lambda i, te, nt: (te[i], 0, 0)),
                  pl.BlockSpec((1, d_ff, D_MODEL), lambda i, te, nt: (te[i], 0, 0))],
        out_specs=row)
    return pl.pallas_call(
        _moe_expert_kernel, grid_spec=grid_spec, out_shape=jax.ShapeDtypeStruct((n_rows, D_MODEL), F32),
        compiler_params=_cparams("arbitrary"), name="moe_experts",
    )(tile_expert, n_tiles_used, xs, w_gate, w_up, w_down)


def _moe_combine_kernel(pos_ref, pos_next_ref, h_ref, gate_ref, ys_hbm, out_ref, buf, sems):
    i = pl.program_id(0)
    n = pl.num_programs(0)
    tm = h_ref.shape[0]
    slot = i % 2

    def copy(p_ref, s, t, k):
        return pltpu.make_async_copy(ys_hbm.at[pl.ds(p_ref[k, t], 1)], buf.at[s, k, pl.ds(t, 1)], sems.at[s])

    def issue(p_ref, s):
        def body(t, c):
            copy(p_ref, s, t, 0).start()
            copy(p_ref, s, t, 1).start()
            return c
        lax.fori_loop(0, tm, body, 0, unroll=8)

    @pl.when(i == 0)
    def _():
        issue(pos_ref, 0)

    @pl.when(i + 1 < n)
    def _():
        issue(pos_next_ref, 1 - slot)

    def drain(t, c):
        copy(pos_ref, slot, t, 0).wait()
        copy(pos_ref, slot, t, 1).wait()
        return c

    lax.fori_loop(0, tm, drain, 0, unroll=8)
    gate = gate_ref[...]
    out_ref[...] = h_ref[...] + gate[:, 0:1] * buf[slot, 0] + gate[:, 1:2] * buf[slot, 1]


def _moe_combine(h2d, gates_t, pos, ys):
    m = h2d.shape[0]
    tm = min(MOE_DMA_CHUNK, m)
    n = m // tm
    row = pl.BlockSpec((tm, D_MODEL), lambda i: (i, 0))
    return pl.pallas_call(
        _moe_combine_kernel, grid=(n,),
        in_specs=[pl.BlockSpec((2, tm), lambda i: (0, i), memory_space=pltpu.SMEM),
                  pl.BlockSpec((2, tm), lambda i: (0, jnp.minimum(i + 1, n - 1)), memory_space=pltpu.SMEM),
                  row, pl.BlockSpec((tm, 2), lambda i: (i, 0)), pl.BlockSpec(memory_space=pl.ANY)],
        out_specs=row, out_shape=jax.ShapeDtypeStruct((m, D_MODEL), F32),
        scratch_shapes=[pltpu.VMEM((2, 2, tm, D_MODEL), F32), pltpu.SemaphoreType.DMA((2,))],
        compiler_params=_cparams("arbitrary"), name="moe_combine",
    )(pos, pos, h2d, gates_t, ys)


def _moe_layer(h2d, norm_g, w_router, b_router, w_gate, w_up, w_down):
    m = h2d.shape[0]
    tg = MOE_ROW_TILE
    xn, idx, gate, rank, cnt = _moe_router(h2d, norm_g, w_router, b_router)
    counts = cnt[:, 0]
    tiles = (counts + tg - 1) // tg
    tile_end = jnp.cumsum(tiles).astype(I32)
    base = (tile_end - tiles) * tg
    pos = rank
    for e in range(N_EXPERTS):
        pos = pos + jnp.where(idx == e, base[e], 0)
    n_tiles = (2 * m) // tg + N_EXPERTS
    tile_expert = jnp.sum((jnp.arange(n_tiles, dtype=I32)[:, None] >= tile_end[None, :]).astype(I32), axis=1)
    tile_expert = jnp.minimum(tile_expert, N_EXPERTS - 1)
    xs = _moe_scatter(xn, pos, tile_end, n_tiles)
    ys = _moe_experts(xs, tile_expert, tile_end[-1:], w_gate, w_up, w_down)
    return _moe_combine(h2d, gate.T, pos, ys)


def kernel(x_prompt, x_sample, cache_k, cache_v, cache_kidx, state_s5_re, state_s5_im, page_table,
           attn_norm_g, attn_w_in, attn_g_q, attn_g_k, attn_g_kidx, attn_w_o,
           dense_norm_g, dense_w_gate, dense_w_up, dense_w_down,
           s5_norm_g, s5_w_in, s5_a_re, s5_a_im, s5_log_dt, s5_b_re, s5_b_im, s5_c_re, s5_c_im,
           s5_d, s5_w_glu,
           moe_norm_g, moe_w_router, moe_b_router, moe_w_gate, moe_w_up, moe_w_down):
    nbp, seq, d = x_prompt.shape
    nbs, t_new, _ = x_sample.shape
    mp, ms = nbp * seq, nbs * t_new
    tp = 8
    xp2 = x_prompt.reshape(mp, d)
    xs2 = x_sample.reshape(ms, d)

    w_in_t = attn_w_in[0].T.astype(BF16)
    o1, o2, o3, o4, o5 = np.cumsum([Q_COLS, KV_COLS, KV_COLS, QI_COLS, IDX_DIM]).tolist()
    w_kw = jnp.pad(w_in_t[o4:], ((0, LANES - (w_in_t.shape[0] - o4)), (0, 0)))
    w_parts = (w_in_t[:o1], w_in_t[o1:o3], w_in_t[o3:o4], w_kw)
    ap = (attn_norm_g[0], w_parts, attn_g_k[0], attn_g_kidx[0])

    q_p, qi_p, wi_p, kt_p, vt_p, kit_p, ktb, vtb, kitb = _attn_proj(xp2, *ap, seq_len=seq)
    o_p = _attn_prompt(q_p, qi_p, wi_p, ktb, vtb, kitb, attn_g_q[0])

    q_s, qi_s, wi_s, k_s, v_s, ki_s = _attn_proj(xs2, *ap)
    pad_t = lambda a: jnp.pad(a.reshape(nbs, t_new, a.shape[-1]), ((0, 0), (0, tp - t_new), (0, 0)))
    n_pool = cache_k.shape[1]
    ck = jnp.transpose(cache_k[0], (0, 2, 3, 1)).reshape(n_pool, KV_COLS, PAGE_SIZE)
    cv = jnp.transpose(cache_v[0], (0, 2, 3, 1)).reshape(n_pool, KV_COLS, PAGE_SIZE)
    cki = jnp.transpose(cache_kidx[0], (0, 2, 1))
    o_s = _attn_sample(pad_t(q_s), pad_t(qi_s.astype(F32)), pad_t(wi_s), pad_t(k_s), pad_t(v_s), pad_t(ki_s),
                       attn_g_q[0], page_table, ck, cv, cki, t_new=t_new)
    o_s = o_s[:, :t_new].reshape(ms, Q_COLS)

    fp = (attn_w_o[0].astype(BF16), dense_norm_g[0], dense_w_gate[0].astype(BF16),
          dense_w_up[0].astype(BF16), dense_w_down[0].astype(BF16))
    hp = _attn_out_ffn(xp2, o_p, *fp)
    hs = _attn_out_ffn(xs2, o_s, *fp)

    abr, abi, bbr, bbi = _s5_params(s5_a_re[0], s5_a_im[0], s5_log_dt[0], s5_b_re[0], s5_b_im[0])
    sp = (s5_norm_g[0], s5_w_in[0].astype(BF16), abr, abi, bbr, bbi, s5_c_re[0], s5_c_im[0], s5_d[0])
    zeros_p = jnp.zeros((nbp, SSM_COLS), F32)
    z_p, srp, sip = _s5_core(hp.reshape(nbp, seq, d), zeros_p, zeros_p, *sp, tt=S5_TIME_TILE)
    z_s, srs, sis = _s5_core(hs.reshape(nbs, t_new, d), state_s5_re[0].reshape(nbs, SSM_COLS),
                             state_s5_im[0].reshape(nbs, SSM_COLS), *sp, tt=t_new)
    w_glu = s5_w_glu[0].astype(BF16)
    hp = _glu_residual(hp, z_p, w_glu)
    hs = _glu_residual(hs, z_s, w_glu)

    mw = (moe_norm_g[0], moe_w_router[0], moe_b_router[0], moe_w_gate[0].astype(BF16),
          moe_w_up[0].astype(BF16), moe_w_down[0].astype(BF16))
    hp = _moe_layer(hp, *mw)
    hs = _moe_layer(hs, *mw)

    fm_kv = lambda a: jnp.transpose(a.reshape(nbp, N_KV_HEADS, HEAD_DIM, seq), (0, 3, 1, 2))[None]
    st = lambda a, n: a.reshape(1, n, SSM_GROUPS, SSM_STATE)
    return (hp.reshape(nbp, seq, d), hs.reshape(nbs, t_new, d),
            fm_kv(kt_p), fm_kv(vt_p), jnp.transpose(kit_p, (0, 2, 1))[None],
            st(srp, nbp), st(sip, nbp),
            k_s.reshape(1, nbs, t_new, N_KV_HEADS, HEAD_DIM), v_s.reshape(1, nbs, t_new, N_KV_HEADS, HEAD_DIM),
            ki_s.reshape(1, nbs, t_new, IDX_DIM),
            st(srs, nbs), st(sis, nbs))
```

```python
import functools
import math

import jax
import jax.numpy as jnp
import numpy as np
from jax import lax
from jax.experimental import pallas as pl
from jax.experimental.pallas import tpu as pltpu

F32 = jnp.float32
BF16 = jnp.bfloat16
I32 = jnp.int32

D_MODEL = 1024
N_HEADS = 16
N_KV_HEADS = 4
HEAD_DIM = 64
Q_PER_KV = N_HEADS // N_KV_HEADS
IDX_HEADS = 16
IDX_DIM = 64
TOPK_MAX = 256
PAGE_SIZE = 128
ATTN_SCALE = HEAD_DIM ** -0.5
INDEX_SCALE = (IDX_HEADS ** -0.5) * (IDX_DIM ** -0.5)
Q_COLS = N_HEADS * HEAD_DIM
KV_COLS = N_KV_HEADS * HEAD_DIM
QI_COLS = IDX_HEADS * IDX_DIM
SSM_GROUP = 16
SSM_GROUPS = D_MODEL // SSM_GROUP
SSM_STATE = 64
SSM_COLS = SSM_GROUPS * SSM_STATE
N_EXPERTS = 8
RMS_EPS = 1e-6

LANES = 128
SUBLANES_BF16 = 16
MXU_DIM = 256
VMEM_LIMIT = 56 * 1024 * 1024
INT_MIN = -(2 ** 31)
NEG_INF_KEY = INT_MIN + 0x7FFFFF

ROW_TILE = 512
ATTN_Q_TILE = 128
ATTN_KEY_BUCKET = 256
ATTN_WIDE_SEARCH_KEYS = 768
S5_TIME_TILE = 32
S5_SCAN_ELEMS = 8192
MOE_ROW_TILE = 256
MOE_DMA_CHUNK = 256


def _cparams(*sem):
    return pltpu.CompilerParams(dimension_semantics=sem, vmem_limit_bytes=VMEM_LIMIT)


def _rms(x, g):
    return x * lax.rsqrt(jnp.mean(x * x, axis=-1, keepdims=True) + RMS_EPS) * g


def _dot(a, b):
    return jnp.dot(a, b, preferred_element_type=F32)


def _dot_nt(a, b):
    return lax.dot_general(a, b, (((1,), (1,)), ((), ())), preferred_element_type=F32)


def _full(shape):
    n = len(shape)
    return pl.BlockSpec(shape, lambda *_: (0,) * n)


def _attn_proj_kernel(x_ref, ng_ref, wq_ref, wkv_ref, wqi_ref, wkw_ref, gk_ref, gki_ref,
                      q_ref, qi_ref, wi_ref, k_ref, v_ref, ki_ref, *rest, transposed):
    xn = _rms(x_ref[...], ng_ref[...]).astype(BF16)
    q_ref[...] = _dot_nt(xn, wq_ref[...])
    qi_ref[...] = _dot_nt(xn, wqi_ref[...]).astype(BF16)
    if transposed:
        kb_ref, vb_ref, kib_ref = rest
        tm = xn.shape[0]
        kv = _dot_nt(wkv_ref[...], xn)
        k = kv[:KV_COLS].reshape(N_KV_HEADS, HEAD_DIM, tm)
        k = k * lax.rsqrt(jnp.mean(k * k, axis=1, keepdims=True) + RMS_EPS) * gk_ref[...][None]
        k = k.reshape(KV_COLS, tm)
        v = kv[KV_COLS:]
        kw = _dot_nt(wkw_ref[...], xn)
        ki = kw[:IDX_DIM]
        ki = ki * lax.rsqrt(jnp.mean(ki * ki, axis=0, keepdims=True) + RMS_EPS) * gki_ref[...]
        wi_ref[...] = kw[IDX_DIM:IDX_DIM + IDX_HEADS].T
        k_ref[0] = k
        v_ref[0] = v
        ki_ref[0] = ki
        kb_ref[0] = k.astype(BF16)
        vb_ref[0] = v.astype(BF16)
        kib_ref[0] = ki.astype(BF16)
    else:
        kv = _dot_nt(xn, wkv_ref[...])
        gk = gk_ref[...]
        ks = []
        for h in range(N_KV_HEADS):
            seg = kv[:, h * HEAD_DIM:(h + 1) * HEAD_DIM]
            ks.append(_rms(seg, gk))
        k_ref[...] = jnp.concatenate(ks, axis=-1)
        v_ref[...] = kv[:, KV_COLS:]
        kw = _dot_nt(xn, wkw_ref[...])
        ki_ref[...] = _rms(kw[:, :IDX_DIM], gki_ref[...])
        wi_ref[...] = kw[:, IDX_DIM:IDX_DIM + IDX_HEADS]


def _attn_proj(x2d, norm_g, w_parts, g_k, g_kidx, *, seq_len=None):
    m = x2d.shape[0]
    tm = min(ROW_TILE, m)
    wq, wkv, wqi, wkw = w_parts
    transposed = seq_len is not None
    row = lambda cols: pl.BlockSpec((tm, cols), lambda i: (i, 0))
    in_specs = [row(D_MODEL), _full((1, D_MODEL)), _full(wq.shape), _full(wkv.shape), _full(wqi.shape),
                _full(wkw.shape)]
    out_shape = [jax.ShapeDtypeStruct((m, Q_COLS), F32), jax.ShapeDtypeStruct((m, QI_COLS), BF16),
                 jax.ShapeDtypeStruct((m, IDX_HEADS), F32)]
    out_specs = [row(Q_COLS), row(QI_COLS), row(IDX_HEADS)]
    if transposed:
        nb, per = m // seq_len, seq_len // tm
        fm = lambda rows: pl.BlockSpec((1, rows, tm), lambda i: (i // per, 0, i % per))
        in_specs += [_full((HEAD_DIM, 1)), _full((IDX_DIM, 1))]
        gk_in, gki_in = g_k.reshape(HEAD_DIM, 1), g_kidx.reshape(IDX_DIM, 1)
        for dt in (F32, BF16):
            out_shape += [jax.ShapeDtypeStruct((nb, KV_COLS, seq_len), dt),
                          jax.ShapeDtypeStruct((nb, KV_COLS, seq_len), dt),
                          jax.ShapeDtypeStruct((nb, IDX_DIM, seq_len), dt)]
            out_specs += [fm(KV_COLS), fm(KV_COLS), fm(IDX_DIM)]
    else:
        in_specs += [_full((1, HEAD_DIM)), _full((1, IDX_DIM))]
        gk_in, gki_in = g_k.reshape(1, HEAD_DIM), g_kidx.reshape(1, IDX_DIM)
        out_shape += [jax.ShapeDtypeStruct((m, KV_COLS), F32), jax.ShapeDtypeStruct((m, KV_COLS), F32),
                      jax.ShapeDtypeStruct((m, IDX_DIM), F32)]
        out_specs += [row(KV_COLS), row(KV_COLS), row(IDX_DIM)]
    return pl.pallas_call(
        functools.partial(_attn_proj_kernel, transposed=transposed),
        grid=(m // tm,), in_specs=in_specs, out_specs=out_specs, out_shape=out_shape,
        compiler_params=_cparams("parallel"), name="attn_proj",
    )(x2d, norm_g.reshape(1, D_MODEL), wq, wkv, wqi, wkw, gk_in, gki_in)


def _key_to_float(key):
    bits = key ^ (lax.shift_right_arithmetic(key, 31) & 0x7FFFFFFF)
    return pltpu.bitcast(bits, F32)


def _count(mask):
    return jnp.sum(jnp.where(mask, 1.0, 0.0), axis=-1, keepdims=True)


def _topk_select(score, pos, k, lim_ref, pos_bits, bits_per_step=1):
    r, s = score.shape
    kf = float(k)

    def count_ge(key):
        return jnp.where(key < NEG_INF_KEY, float(s), _count(score >= _key_to_float(key)))

    thr = jnp.where(_count(score >= 0.0) >= kf, 0, INT_MIN).astype(I32)
    hi = 30
    while hi >= 0:
        nbits = min(bits_per_step, hi + 1)
        lo = hi - nbits + 1
        passed = [jnp.where(count_ge(thr | (j << lo)) >= kf, 1, 0) for j in range(1, 2 ** nbits)]
        thr = thr | (sum(passed).astype(I32) << lo)
        hi = lo - 1
    thr_f = _key_to_float(thr)
    above = score > thr_f
    tied = score == thr_f
    take = kf - _count(above)
    lim_ref[...] = jnp.full((r, 1), 2 ** pos_bits, I32)
    excess = (_count(tied) > take) & (thr > NEG_INF_KEY)

    @pl.when(jnp.max(jnp.where(excess, 1.0, 0.0)) > 0.0)
    def _():
        def pos_bit(i, lo):
            cand = lo | lax.shift_left(jnp.int32(1), pos_bits - 1 - i)
            return jnp.where(_count(tied & (pos < cand)) < take, cand, lo)

        lim_ref[...] = lax.fori_loop(0, pos_bits, pos_bit, jnp.zeros((r, 1), I32))

    return above | (tied & (pos <= lim_ref[...]))


def _norm_q_heads(q, gq):
    scale = ATTN_SCALE * math.log2(math.e)
    return [_rms(q[:, h * HEAD_DIM:(h + 1) * HEAD_DIM], gq) * scale for h in range(N_HEADS)]


def _attn_prompt_kernel(q_ref, qi_ref, wi_ref, kt_ref, vt_ref, kit_ref, gq_ref, o_ref, lim_ref, *, tq, ksel):
    j = pl.program_id(1)
    seq = kt_ref.shape[2]
    n_buckets = seq // ATTN_KEY_BUCKET
    per_bucket = ATTN_KEY_BUCKET // tq

    def block(s):
        q_pos = j * tq + lax.broadcasted_iota(I32, (tq, 1), 0)
        key_pos = lax.broadcasted_iota(I32, (1, s), 1)
        causal = key_pos <= q_pos
        qi = qi_ref[...]
        wi = wi_ref[...]
        qi_rows = jnp.concatenate([qi[:, h * IDX_DIM:(h + 1) * IDX_DIM] for h in range(IDX_HEADS)], axis=0)
        tiles = []
        for c0 in range(0, s, MXU_DIM):
            sh = _dot(qi_rows, kit_ref[0, :, c0:c0 + MXU_DIM])
            acc = wi[:, 0:1] * jnp.maximum(sh[:tq], 0.0)
            for h in range(1, IDX_HEADS):
                acc = acc + wi[:, h:h + 1] * jnp.maximum(sh[h * tq:(h + 1) * tq], 0.0)
            tiles.append(acc)
        score = jnp.where(causal, jnp.concatenate(tiles, axis=1) * INDEX_SCALE, -jnp.inf)

        heads = _norm_q_heads(q_ref[...], gq_ref[...])
        raw = []
        for n in range(N_KV_HEADS):
            qn = jnp.concatenate(heads[n * Q_PER_KV:(n + 1) * Q_PER_KV], axis=0).astype(BF16)
            raw.append(_dot(qn, kt_ref[0, n * HEAD_DIM:(n + 1) * HEAD_DIM, :s]))

        sel = _topk_select(score, key_pos, ksel, lim_ref, pos_bits=int(math.log2(seq)),
                           bits_per_step=2 if s <= ATTN_WIDE_SEARCH_KEYS else 1)
        bias = jnp.where(sel & causal, 0.0, -jnp.inf)[None]

        outs = [None] * N_HEADS
        ones = jnp.ones((SUBLANES_BF16, s), BF16)
        for n in range(N_KV_HEADS):
            vn = jnp.concatenate([vt_ref[0, n * HEAD_DIM:(n + 1) * HEAD_DIM, :s], ones], axis=0)
            logits = raw[n].reshape(Q_PER_KV, tq, s) + bias
            m = jnp.max(logits, axis=-1, keepdims=True)
            p = jnp.exp2(logits - m)
            pv = _dot_nt(p.reshape(Q_PER_KV * tq, s).astype(BF16), vn)
            pv = pv[:, :HEAD_DIM] / pv[:, HEAD_DIM:HEAD_DIM + 1]
            for g in range(Q_PER_KV):
                outs[n * Q_PER_KV + g] = pv[g * tq:(g + 1) * tq]
        o_ref[...] = jnp.concatenate(outs, axis=-1).astype(BF16)

    for e in range(n_buckets):
        pl.when(j // per_bucket == e)(functools.partial(block, (e + 1) * ATTN_KEY_BUCKET))


def _attn_prompt(q, qi, wi, ktb, vtb, kitb, g_q):
    nb, _, seq = ktb.shape
    tq = ATTN_Q_TILE
    nq = seq // tq
    ksel = min(TOPK_MAX, seq // 4)
    row = lambda cols: pl.BlockSpec((tq, cols), lambda b, j: (b * nq + j, 0))
    fm = lambda rows: pl.BlockSpec((1, rows, seq), lambda b, j: (b, 0, 0))
    return pl.pallas_call(
        functools.partial(_attn_prompt_kernel, tq=tq, ksel=ksel),
        grid=(nb, nq),
        in_specs=[row(Q_COLS), row(QI_COLS), row(IDX_HEADS), fm(KV_COLS), fm(KV_COLS), fm(IDX_DIM),
                  _full((1, HEAD_DIM))],
        out_specs=row(Q_COLS),
        out_shape=jax.ShapeDtypeStruct((nb * seq, Q_COLS), BF16),
        scratch_shapes=[pltpu.VMEM((tq, 1), I32)],
        compiler_params=_cparams("parallel", "arbitrary"), name="attn_prompt",
    )(q, qi, wi, ktb, vtb, kitb, g_q.reshape(1, HEAD_DIM))


def _attn_sample_kernel(pt_ref, q_ref, qi_ref, wi_ref, kn_ref, vn_ref, kin_ref, gq_ref,
                        ck_hbm, cv_hbm, cki_hbm, o_ref, kbuf, vbuf, kibuf, sems, lim_ref,
                        *, n_pages, t_new, ksel):
    b = pl.program_id(0)
    nb = pl.num_programs(0)
    past = n_pages * PAGE_SIZE
    tp = q_ref.shape[1]

    def page_copies(batch, slot, p):
        page = pt_ref[batch, p]
        cols = pl.ds(p * PAGE_SIZE, PAGE_SIZE)
        return (pltpu.make_async_copy(ck_hbm.at[page], kbuf.at[slot, :, cols], sems.at[0, slot]),
                pltpu.make_async_copy(cv_hbm.at[page], vbuf.at[slot, :, cols], sems.at[1, slot]),
                pltpu.make_async_copy(cki_hbm.at[page], kibuf.at[slot, :, cols], sems.at[2, slot]))

    def fetch(batch, slot):
        def issue(p, c):
            for cp in page_copies(batch, slot, p):
                cp.start()
            return c
        lax.fori_loop(0, n_pages, issue, 0)

    def wait(batch, slot):
        def done(p, c):
            for cp in page_copies(batch, slot, p):
                cp.wait()
            return c
        lax.fori_loop(0, n_pages, done, 0)

    slot = b % 2

    @pl.when(b == 0)
    def _():
        fetch(0, 0)

    @pl.when(b + 1 < nb)
    def _():
        fetch(b + 1, 1 - slot)

    wait(b, slot)

    nrow = N_HEADS * tp
    s_all = past + LANES

    def new_rows(ref):
        x = ref[0]
        return jnp.concatenate([x, jnp.zeros((LANES - tp, x.shape[1]), F32)], axis=0).astype(BF16)

    qi = qi_ref[0]
    wi = wi_ref[0]
    qi_rows = jnp.concatenate([qi[:, h * IDX_DIM:(h + 1) * IDX_DIM] for h in range(IDX_HEADS)],
                              axis=0).astype(BF16)
    wi_rows = jnp.concatenate([wi[:, h:h + 1] for h in range(IDX_HEADS)], axis=0)
    s_idx = jnp.concatenate([_dot(qi_rows, kibuf[slot].astype(BF16)),
                             _dot_nt(qi_rows, new_rows(kin_ref))], axis=1)
    s_idx = wi_rows * jnp.maximum(s_idx, 0.0)
    score = jnp.sum(s_idx.reshape(IDX_HEADS, tp, s_all), axis=0) * INDEX_SCALE
    q_pos = past + lax.broadcasted_iota(I32, (tp, 1), 0)
    key_pos = lax.broadcasted_iota(I32, (1, s_all), 1)
    causal = (key_pos <= q_pos) & (key_pos < past + t_new)
    score = jnp.where(causal, score, -jnp.inf)

    heads = _norm_q_heads(q_ref[0], gq_ref[...])
    zero = jnp.zeros((Q_PER_KV * tp, HEAD_DIM), F32)
    qbd = jnp.concatenate([
        jnp.concatenate([jnp.concatenate(heads[n * Q_PER_KV:(n + 1) * Q_PER_KV], axis=0) if c == n else zero
                         for c in range(N_KV_HEADS)], axis=1)
        for n in range(N_KV_HEADS)], axis=0).astype(BF16)
    logits = jnp.concatenate([_dot(qbd, kbuf[slot].astype(BF16)),
                              _dot_nt(qbd, new_rows(kn_ref))], axis=1)

    sel = _topk_select(score, key_pos, ksel, lim_ref, pos_bits=int(math.ceil(math.log2(s_all))),
                       bits_per_step=3)
    bias = jnp.where(sel & causal, 0.0, -jnp.inf)[None]
    logits = (logits.reshape(N_HEADS, tp, s_all) + bias).reshape(nrow, s_all)
    m = jnp.max(logits, axis=-1, keepdims=True)
    p = jnp.exp2(logits - m)
    l = jnp.sum(p, axis=-1, keepdims=True)
    pb = p.astype(BF16)
    pv = _dot_nt(pb[:, :past], vbuf[slot].astype(BF16)) + _dot(pb[:, past:], new_rows(vn_ref))
    pv = pv / l
    outs = []
    for n in range(N_KV_HEADS):
        for g in range(Q_PER_KV):
            r0 = (n * Q_PER_KV + g) * tp
            outs.append(pv[r0:r0 + tp, n * HEAD_DIM:(n + 1) * HEAD_DIM])
    o_ref[0] = jnp.concatenate(outs, axis=-1).astype(BF16)


def _attn_sample(q, qi, wi, k_new, v_new, ki_new, g_q, page_table, ck, cv, cki, *, t_new):
    nb, tp, _ = q.shape
    n_pages = page_table.shape[1]
    past = n_pages * PAGE_SIZE
    ksel = min(TOPK_MAX, (past + t_new) // 4)
    blk = lambda cols: pl.BlockSpec((1, tp, cols), lambda b, pt: (b, 0, 0))
    grid_spec = pltpu.PrefetchScalarGridSpec(
        num_scalar_prefetch=1, grid=(nb,),
        in_specs=[blk(Q_COLS), blk(QI_COLS), blk(IDX_HEADS), blk(KV_COLS), blk(KV_COLS), blk(IDX_DIM),
                  pl.BlockSpec((1, HEAD_DIM), lambda b, pt: (0, 0)),
                  pl.BlockSpec(memory_space=pl.ANY), pl.BlockSpec(memory_space=pl.ANY),
                  pl.BlockSpec(memory_space=pl.ANY)],
        out_specs=blk(Q_COLS),
        scratch_shapes=[pltpu.VMEM((2, KV_COLS, past), F32), pltpu.VMEM((2, KV_COLS, past), F32),
                        pltpu.VMEM((2, IDX_DIM, past), F32), pltpu.SemaphoreType.DMA((3, 2)),
                        pltpu.VMEM((tp, 1), I32)])
    return pl.pallas_call(
        functools.partial(_attn_sample_kernel, n_pages=n_pages, t_new=t_new, ksel=ksel),
        grid_spec=grid_spec, out_shape=jax.ShapeDtypeStruct((nb, tp, Q_COLS), BF16),
        compiler_params=_cparams("arbitrary"), name="attn_sample",
    )(page_table, q, qi, wi, k_new, v_new, ki_new, g_q.reshape(1, HEAD_DIM), ck, cv, cki)


def _attn_out_ffn_kernel(x_ref, o_ref, wo_ref, ng_ref, wg_ref, wu_ref, wd_ref, out_ref):
    h = x_ref[...] + _dot(o_ref[...], wo_ref[...])
    xn = _rms(h, ng_ref[...]).astype(BF16)
    gate = _dot(xn, wg_ref[...])
    up = _dot(xn, wu_ref[...])
    act = (gate * jax.nn.sigmoid(gate) * up).astype(BF16)
    out_ref[...] = h + _dot(act, wd_ref[...])


def _resident(shape):
    n = len(shape)
    return pl.BlockSpec(shape, lambda *_: (0,) * n, pipeline_mode=pl.Buffered(1))


def _attn_out_ffn(x2d, o, w_o, norm_g, w_gate, w_up, w_down):
    m = x2d.shape[0]
    tm = min(ROW_TILE, m)
    row = lambda cols: pl.BlockSpec((tm, cols), lambda i: (i, 0))
    return pl.pallas_call(
        _attn_out_ffn_kernel, grid=(m // tm,),
        in_specs=[row(D_MODEL), row(Q_COLS), _resident(w_o.shape), _full((1, D_MODEL)),
                  _resident(w_gate.shape), _resident(w_up.shape), _resident(w_down.shape)],
        out_specs=row(D_MODEL), out_shape=jax.ShapeDtypeStruct((m, D_MODEL), F32),
        compiler_params=_cparams("parallel"), name="attn_out_ffn",
    )(x2d, o, w_o, norm_g.reshape(1, D_MODEL), w_gate, w_up, w_down)


def _s5_params_kernel(are_ref, aim_ref, ldt_ref, bre_ref, bim_ref, abr_ref, abi_ref, bbr_ref, bbi_ref):
    dt = jnp.exp(ldt_ref[...])
    lr = are_ref[...] * dt
    li = aim_ref[...] * dt
    mag = jnp.exp(lr)
    abr = mag * jnp.cos(li)
    abi = mag * jnp.sin(li)
    den = lr * lr + li * li
    fr = ((abr - 1.0) * lr + abi * li) / den * dt
    fi = (abi * lr - (abr - 1.0) * li) / den * dt
    abr_ref[...] = abr
    abi_ref[...] = abi
    bre = bre_ref[...]
    bim = bim_ref[...]
    bbr_ref[...] = fr * bre - fi * bim
    bbi_ref[...] = fr * bim + fi * bre


def _s5_params(a_re, a_im, log_dt, b_re, b_im):
    g, p, c = b_re.shape
    bt = lambda b: jnp.transpose(b, (0, 2, 1))
    shp3 = jax.ShapeDtypeStruct((g, 1, p), F32)
    shpb = jax.ShapeDtypeStruct((g, c, p), F32)
    return pl.pallas_call(
        _s5_params_kernel, out_shape=[shp3, shp3, shpb, shpb], name="s5_params",
    )(a_re.reshape(g, 1, p), a_im.reshape(g, 1, p), log_dt.reshape(g, 1, 1), bt(b_re), bt(b_im))


def _block_diag_tiles(w, groups_per_tile):
    g, a, b = w.shape
    eye = jnp.eye(groups_per_tile, dtype=w.dtype)
    w = w.reshape(g // groups_per_tile, groups_per_tile, a, b)
    out = w[:, :, :, None, :] * eye[None, :, None, :, None]
    return out.reshape(g // groups_per_tile, groups_per_tile * a, groups_per_tile * b)


def _s5_core_kernel(x_ref, ng_ref, win_ref, perm_ref, permt_ref, bbr_ref, bbi_ref, cr_ref, ci_ref,
                    abr_ref, abi_ref, d_ref, s0r_ref, s0i_ref, z_ref, sr_ref, si_ref, *x_refs,
                    nb, tt):
    i = pl.program_id(0)
    rows = nb * tt
    gpt = MXU_DIM // SSM_GROUP
    n_kt = D_MODEL // MXU_DIM
    st_cols = gpt * SSM_STATE
    xr_refs, xi_refs = x_refs[:n_kt], x_refs[n_kt:]

    @pl.when(i == 0)
    def _():
        sr_ref[...] = s0r_ref[...]
        si_ref[...] = s0i_ref[...]

    x = x_ref[...].reshape(rows, D_MODEL)
    xn = _rms(x, ng_ref[...]).astype(BF16)
    xn = _dot(perm_ref[...], xn).astype(BF16)
    u = _dot(xn, win_ref[...])
    ub = u.astype(BF16)
    sr0, si0 = sr_ref[...], si_ref[...]

    cw = min(S5_SCAN_ELEMS // nb, st_cols)
    ys, srs, sis = [], [], []
    for kt in range(n_kt):
        xr_ref, xi_ref = xr_refs[kt], xi_refs[kt]
        uk = ub[:, kt * MXU_DIM:(kt + 1) * MXU_DIM]
        xr_ref[...] = _dot(uk, bbr_ref[kt])
        xi_ref[...] = _dot(uk, bbi_ref[kt])
        for c0 in range(0, st_cols, cw):
            g0 = kt * st_cols + c0
            ar = jnp.broadcast_to(abr_ref[:, g0:g0 + cw], (nb, cw))
            ai = jnp.broadcast_to(abi_ref[:, g0:g0 + cw], (nb, cw))
            hr, hi = sr0[:, g0:g0 + cw], si0[:, g0:g0 + cw]
            for t in range(tt):
                r = slice(t * nb, (t + 1) * nb)
                hr, hi = (ar * hr - ai * hi + xr_ref[r, c0:c0 + cw],
                          ar * hi + ai * hr + xi_ref[r, c0:c0 + cw])
                xr_ref[r, c0:c0 + cw] = hr
                xi_ref[r, c0:c0 + cw] = hi
            srs.append(hr)
            sis.append(hi)
        ys.append(_dot(xr_ref[...].astype(BF16), cr_ref[kt]) - _dot(xi_ref[...].astype(BF16), ci_ref[kt]))
    sr_ref[...] = jnp.concatenate(srs, axis=-1)
    si_ref[...] = jnp.concatenate(sis, axis=-1)
    y = jnp.concatenate(ys, axis=-1) + d_ref[...] * u
    z = jax.nn.gelu(y).astype(BF16)
    z_ref[...] = _dot(permt_ref[...], z).astype(BF16).reshape(z_ref.shape)


def _s5_core(x, s0_re, s0_im, norm_g, w_in, abr, abi, bbr, bbi, c_re, c_im, d_skip, *, tt):
    nb, t, _ = x.shape
    rows = nb * tt
    gpt = MXU_DIM // SSM_GROUP
    r = np.arange(rows)
    perm_np = np.zeros((rows, rows), np.float32)
    perm_np[r, (r % nb) * tt + r // nb] = 1.0
    perm = jnp.asarray(perm_np, BF16)
    permt = jnp.asarray(perm_np.T, BF16)
    bbr_t = _block_diag_tiles(bbr, gpt).astype(BF16)
    bbi_t = _block_diag_tiles(bbi, gpt).astype(BF16)
    ct = lambda c: _block_diag_tiles(jnp.transpose(c, (0, 2, 1)), gpt).astype(BF16)
    if tt % 8 == 0:
        x_in, x_spec = x, pl.BlockSpec((nb, tt, D_MODEL), lambda i: (0, i, 0))
    else:
        assert tt == t
        x_in, x_spec = x.reshape(rows, D_MODEL), pl.BlockSpec((rows, D_MODEL), lambda i: (0, 0))
    z_spec = (pl.BlockSpec((nb, tt, D_MODEL), lambda i: (0, i, 0)) if tt % 8 == 0
              else pl.BlockSpec((rows, D_MODEL), lambda i: (0, 0)))
    z_shape = (nb, t, D_MODEL) if tt % 8 == 0 else (rows, D_MODEL)
    st = jax.ShapeDtypeStruct((nb, SSM_COLS), F32)
    z, sr, si = pl.pallas_call(
        functools.partial(_s5_core_kernel, nb=nb, tt=tt), grid=(t // tt,),
        in_specs=[x_spec, _full((1, D_MODEL)), _full(w_in.shape), _full(perm.shape), _full(permt.shape),
                  _full(bbr_t.shape), _full(bbi_t.shape), _full((D_MODEL // MXU_DIM, gpt * SSM_STATE, MXU_DIM)),
                  _full((D_MODEL // MXU_DIM, gpt * SSM_STATE, MXU_DIM)),
                  _full((1, SSM_COLS)), _full((1, SSM_COLS)), _full((1, D_MODEL)),
                  _full((nb, SSM_COLS)), _full((nb, SSM_COLS))],
        out_specs=[z_spec, _full((nb, SSM_COLS)), _full((nb, SSM_COLS))],
        out_shape=[jax.ShapeDtypeStruct(z_shape, BF16), st, st],
        scratch_shapes=[pltpu.VMEM((rows, gpt * SSM_STATE), F32)] * (2 * (D_MODEL // MXU_DIM)),
        compiler_params=_cparams("arbitrary"), name="s5_core",
    )(x_in, norm_g.reshape(1, D_MODEL), w_in, perm, permt, bbr_t, bbi_t, ct(c_re), ct(c_im),
      abr.reshape(1, SSM_COLS), abi.reshape(1, SSM_COLS), d_skip.reshape(1, D_MODEL), s0_re, s0_im)
    return z.reshape(nb * t, D_MODEL), sr, si


def _glu_kernel(h_ref, z_ref, w_ref, out_ref):
    gl = _dot(z_ref[...], w_ref[...])
    out_ref[...] = h_ref[...] + gl[:, :D_MODEL] * jax.nn.sigmoid(gl[:, D_MODEL:])


def _glu_residual(h2d, z, w_glu):
    m = h2d.shape[0]
    tm = min(ROW_TILE, m)
    row = pl.BlockSpec((tm, D_MODEL), lambda i: (i, 0))
    return pl.pallas_call(
        _glu_kernel, grid=(m // tm,), in_specs=[row, row, _full(w_glu.shape)], out_specs=row,
        out_shape=jax.ShapeDtypeStruct((m, D_MODEL), F32), compiler_params=_cparams("parallel"), name="s5_glu",
    )(h2d, z, w_glu)


def _moe_router_kernel(h_ref, ng_ref, wr_ref, br_ref, xn_ref, idx_ref, gate_ref, rank_ref, cnt_ref, carry_ref):
    i = pl.program_id(0)
    tm = h_ref.shape[0]

    @pl.when(i == 0)
    def _():
        carry_ref[...] = jnp.zeros_like(carry_ref)

    xn = _rms(h_ref[...], ng_ref[...])
    xn_ref[...] = xn
    logits = lax.dot_general(wr_ref[...], xn, (((1,), (1,)), ((), ())), preferred_element_type=F32,
                             precision=lax.Precision.HIGHEST) + br_ref[...]
    eid = lax.broadcasted_iota(I32, (N_EXPERTS, tm), 0).astype(F32)
    m1 = jnp.max(logits, axis=0, keepdims=True)
    i1 = jnp.min(jnp.where(logits == m1, eid, float(N_EXPERTS)), axis=0, keepdims=True)
    rest = jnp.where(eid == i1, -jnp.inf, logits)
    m2 = jnp.max(rest, axis=0, keepdims=True)
    i2 = jnp.min(jnp.where(rest == m2, eid, float(N_EXPERTS)), axis=0, keepdims=True)
    e2 = jnp.exp(m2 - m1)
    den = 1.0 + e2
    idx_ref[...] = jnp.concatenate([i1, i2], axis=0).astype(I32)
    gate_ref[...] = jnp.concatenate([1.0 / den, e2 / den], axis=0)

    onehot = jnp.where((eid == i1) | (eid == i2), 1.0, 0.0)
    src = lax.broadcasted_iota(I32, (tm, tm), 0)
    dst = lax.broadcasted_iota(I32, (tm, tm), 1)
    tri = jnp.where(src <= dst, 1.0, 0.0).astype(BF16)
    cum = _dot(onehot.astype(BF16), tri) + carry_ref[:, 0:1]
    before = cum - 1.0
    r1 = jnp.sum(jnp.where(eid == i1, before, 0.0), axis=0, keepdims=True)
    r2 = jnp.sum(jnp.where(eid == i2, before, 0.0), axis=0, keepdims=True)
    rank_ref[...] = jnp.concatenate([r1, r2], axis=0).astype(I32)
    total = cum[:, tm - 1:tm]
    carry_ref[...] = jnp.broadcast_to(total, carry_ref.shape)
    cnt_ref[...] = jnp.broadcast_to(total, cnt_ref.shape).astype(I32)


def _moe_router(h2d, norm_g, w_router, b_router):
    m = h2d.shape[0]
    tm = min(ROW_TILE, m)
    row = pl.BlockSpec((tm, D_MODEL), lambda i: (i, 0))
    col2 = pl.BlockSpec((2, tm), lambda i: (0, i))
    return pl.pallas_call(
        _moe_router_kernel, grid=(m // tm,),
        in_specs=[row, _full((1, D_MODEL)), _full((N_EXPERTS, D_MODEL)), _full((N_EXPERTS, 1))],
        out_specs=[row, col2, col2, col2, _full((N_EXPERTS, LANES))],
        out_shape=[jax.ShapeDtypeStruct((m, D_MODEL), F32), jax.ShapeDtypeStruct((2, m), I32),
                   jax.ShapeDtypeStruct((2, m), F32), jax.ShapeDtypeStruct((2, m), I32),
                   jax.ShapeDtypeStruct((N_EXPERTS, LANES), I32)],
        scratch_shapes=[pltpu.VMEM((N_EXPERTS, LANES), F32)],
        compiler_params=_cparams("arbitrary"), name="moe_router",
    )(h2d, norm_g.reshape(1, D_MODEL), w_router.T, b_router.reshape(N_EXPERTS, 1))


def _moe_scatter_kernel(tend_ref, pos_ref, xn_ref, xs_hbm, zbuf, sem, zsem, *, n_tiles):
    i = pl.program_id(0)
    ch = pos_ref.shape[1]
    tg = zbuf.shape[0]

    @pl.when(i == 0)
    def _():
        zbuf[...] = jnp.zeros_like(zbuf)
        used = tend_ref[N_EXPERTS - 1]

        def clears():
            for e in range(N_EXPERTS):
                first = tend_ref[e - 1] if e else 0
                yield tend_ref[e] > first, tend_ref[e] - 1
            for r in range(N_EXPERTS):
                yield used + r < n_tiles, used + r

        for go, tile in clears():
            @pl.when(go)
            def _():
                pltpu.make_async_copy(zbuf, xs_hbm.at[pl.ds(tile * tg, tg)], zsem).start()
        for go, tile in clears():
            @pl.when(go)
            def _():
                pltpu.make_async_copy(zbuf, xs_hbm.at[pl.ds(tile * tg, tg)], zsem).wait()

    def copy(t, k):
        return pltpu.make_async_copy(xn_ref.at[pl.ds(t, 1)], xs_hbm.at[pl.ds(pos_ref[k, t], 1)], sem)

    def issue(t, c):
        copy(t, 0).start()
        copy(t, 1).start()
        return c

    def drain(t, c):
        copy(t, 0).wait()
        copy(t, 1).wait()
        return c

    lax.fori_loop(0, ch, issue, 0, unroll=8)
    lax.fori_loop(0, ch, drain, 0, unroll=8)


def _moe_scatter(xn, pos, tile_end, n_tiles):
    m = xn.shape[0]
    ch = min(MOE_DMA_CHUNK, m)
    tg = MOE_ROW_TILE
    grid_spec = pltpu.PrefetchScalarGridSpec(
        num_scalar_prefetch=1, grid=(m // ch,),
        in_specs=[pl.BlockSpec((2, ch), lambda i, te: (0, i), memory_space=pltpu.SMEM),
                  pl.BlockSpec((ch, D_MODEL), lambda i, te: (i, 0))],
        out_specs=pl.BlockSpec(memory_space=pl.ANY),
        scratch_shapes=[pltpu.VMEM((tg, D_MODEL), F32), pltpu.SemaphoreType.DMA(()),
                        pltpu.SemaphoreType.DMA(())])
    return pl.pallas_call(
        functools.partial(_moe_scatter_kernel, n_tiles=n_tiles), grid_spec=grid_spec,
        out_shape=jax.ShapeDtypeStruct((n_tiles * tg, D_MODEL), F32),
        compiler_params=_cparams("arbitrary"), name="moe_scatter",
    )(tile_end, pos, xn)


def _moe_expert_kernel(te_ref, nt_ref, xs_ref, wg_ref, wu_ref, wd_ref, ys_ref):
    i = pl.program_id(0)

    @pl.when(i < nt_ref[0])
    def _():
        x = xs_ref[...].astype(BF16)
        gate = _dot(x, wg_ref[0])
        up = _dot(x, wu_ref[0])
        act = (gate * jax.nn.sigmoid(gate) * up).astype(BF16)
        ys_ref[...] = _dot(act, wd_ref[0])

    @pl.when(i >= nt_ref[0])
    def _():
        ys_ref[...] = jnp.zeros_like(ys_ref)


def _moe_experts(xs, tile_expert, n_tiles_used, w_gate, w_up, w_down):
    n_rows = xs.shape[0]
    tg = MOE_ROW_TILE
    d_ff = w_gate.shape[2]
    row = pl.BlockSpec((tg, D_MODEL), lambda i, te, nt: (i, 0))
    grid_spec = pltpu.PrefetchScalarGridSpec(
        num_scalar_prefetch=2, grid=(n_rows // tg,),
        in_specs=[row,
                  pl.BlockSpec((1, D_MODEL, d_ff), lambda i, te, nt: (te[i], 0, 0)),
                  pl.BlockSpec((1, D_MODEL, d_ff), lambda i, te, nt: (te[i], 0, 0)),
                  pl.BlockSpec((1, d_ff, D_MODEL), lambda i, te, nt: (te[i], 0, 0))],
        out_specs=row)
    return pl.pallas_call(
        _moe_expert_kernel, grid_spec=grid_spec, out_shape=jax.ShapeDtypeStruct((n_rows, D_MODEL), F32),
        compiler_params=_cparams("arbitrary"), name="moe_experts",
    )(tile_expert, n_tiles_used, xs, w_gate, w_up, w_down)


def _moe_combine_kernel(pos_ref, pos_next_ref, h_ref, gate_ref, ys_hbm, out_ref, buf, sems):
    i = pl.program_id(0)
    n = pl.num_programs(0)
    tm = h_ref.shape[0]
    slot = i % 2

    def copy(p_ref, s, t, k):
        return pltpu.make_async_copy(ys_hbm.at[pl.ds(p_ref[k, t], 1)], buf.at[s, k, pl.ds(t, 1)], sems.at[s])

    def issue(p_ref, s):
        def body(t, c):
            copy(p_ref, s, t, 0).start()
            copy(p_ref, s, t, 1).start()
            return c
        lax.fori_loop(0, tm, body, 0, unroll=8)

    @pl.when(i == 0)
    def _():
        issue(pos_ref, 0)

    @pl.when(i + 1 < n)
    def _():
        issue(pos_next_ref, 1 - slot)

    def drain(t, c):
        copy(pos_ref, slot, t, 0).wait()
        copy(pos_ref, slot, t, 1).wait()
        return c

    lax.fori_loop(0, tm, drain, 0, unroll=8)
    gate = gate_ref[...]
    out_ref[...] = h_ref[...] + gate[:, 0:1] * buf[slot, 0] + gate[:, 1:2] * buf[slot, 1]


def _moe_combine(h2d, gates_t, pos, ys):
    m = h2d.shape[0]
    tm = min(MOE_DMA_CHUNK, m)
    n = m // tm
    row = pl.BlockSpec((tm, D_MODEL), lambda i: (i, 0))
    return pl.pallas_call(
        _moe_combine_kernel, grid=(n,),
        in_specs=[pl.BlockSpec((2, tm), lambda i: (0, i), memory_space=pltpu.SMEM),
                  pl.BlockSpec((2, tm), lambda i: (0, jnp.minimum(i + 1, n - 1)), memory_space=pltpu.SMEM),
                  row, pl.BlockSpec((tm, 2), lambda i: (i, 0)), pl.BlockSpec(memory_space=pl.ANY)],
        out_specs=row, out_shape=jax.ShapeDtypeStruct((m, D_MODEL), F32),
        scratch_shapes=[pltpu.VMEM((2, 2, tm, D_MODEL), F32), pltpu.SemaphoreType.DMA((2,))],
        compiler_params=_cparams("arbitrary"), name="moe_combine",
    )(pos, pos, h2d, gates_t, ys)


def _moe_layer(h2d, norm_g, w_router, b_router, w_gate, w_up, w_down):
    m = h2d.shape[0]
    tg = MOE_ROW_TILE
    xn, idx, gate, rank, cnt = _moe_router(h2d, norm_g, w_router, b_router)
    counts = cnt[:, 0]
    tiles = (counts + tg - 1) // tg
    tile_end = jnp.cumsum(tiles).astype(I32)
    base = (tile_end - tiles) * tg
    pos = rank
    for e in range(N_EXPERTS):
        pos = pos + jnp.where(idx == e, base[e], 0)
    n_tiles = (2 * m) // tg + N_EXPERTS
    tile_expert = jnp.sum((jnp.arange(n_tiles, dtype=I32)[:, None] >= tile_end[None, :]).astype(I32), axis=1)
    tile_expert = jnp.minimum(tile_expert, N_EXPERTS - 1)
    xs = _moe_scatter(xn, pos, tile_end, n_tiles)
    ys = _moe_experts(xs, tile_expert, tile_end[-1:], w_gate, w_up, w_down)
    return _moe_combine(h2d, gate.T, pos, ys)


def kernel(x_prompt, x_sample, cache_k, cache_v, cache_kidx, state_s5_re, state_s5_im, page_table,
           attn_norm_g, attn_w_in, attn_g_q, attn_g_k, attn_g_kidx, attn_w_o,
           dense_norm_g, dense_w_gate, dense_w_up, dense_w_down,
           s5_norm_g, s5_w_in, s5_a_re, s5_a_im, s5_log_dt, s5_b_re, s5_b_im, s5_c_re, s5_c_im,
           s5_d, s5_w_glu,
           moe_norm_g, moe_w_router, moe_b_router, moe_w_gate, moe_w_up, moe_w_down):
    nbp, seq, d = x_prompt.shape
    nbs, t_new, _ = x_sample.shape
    mp, ms = nbp * seq, nbs * t_new
    tp = 8
    xp2 = x_prompt.reshape(mp, d)
    xs2 = x_sample.reshape(ms, d)

    w_in_t = attn_w_in[0].T.astype(BF16)
    o1, o2, o3, o4, o5 = np.cumsum([Q_COLS, KV_COLS, KV_COLS, QI_COLS, IDX_DIM]).tolist()
    w_kw = jnp.pad(w_in_t[o4:], ((0, LANES - (w_in_t.shape[0] - o4)), (0, 0)))
    w_parts = (w_in_t[:o1], w_in_t[o1:o3], w_in_t[o3:o4], w_kw)
    ap = (attn_norm_g[0], w_parts, attn_g_k[0], attn_g_kidx[0])

    q_p, qi_p, wi_p, kt_p, vt_p, kit_p, ktb, vtb, kitb = _attn_proj(xp2, *ap, seq_len=seq)
    o_p = _attn_prompt(q_p, qi_p, wi_p, ktb, vtb, kitb, attn_g_q[0])

    q_s, qi_s, wi_s, k_s, v_s, ki_s = _attn_proj(xs2, *ap)
    pad_t = lambda a: jnp.pad(a.reshape(nbs, t_new, a.shape[-1]), ((0, 0), (0, tp - t_new), (0, 0)))
    n_pool = cache_k.shape[1]
    ck = jnp.transpose(cache_k[0], (0, 2, 3, 1)).reshape(n_pool, KV_COLS, PAGE_SIZE)
    cv = jnp.transpose(cache_v[0], (0, 2, 3, 1)).reshape(n_pool, KV_COLS, PAGE_SIZE)
    cki = jnp.transpose(cache_kidx[0], (0, 2, 1))
    o_s = _attn_sample(pad_t(q_s), pad_t(qi_s.astype(F32)), pad_t(wi_s), pad_t(k_s), pad_t(v_s), pad_t(ki_s),
                       attn_g_q[0], page_table, ck, cv, cki, t_new=t_new)
    o_s = o_s[:, :t_new].reshape(ms, Q_COLS)

    fp = (attn_w_o[0].astype(BF16), dense_norm_g[0], dense_w_gate[0].astype(BF16),
          dense_w_up[0].astype(BF16), dense_w_down[0].astype(BF16))
    hp = _attn_out_ffn(xp2, o_p, *fp)
    hs = _attn_out_ffn(xs2, o_s, *fp)

    abr, abi, bbr, bbi = _s5_params(s5_a_re[0], s5_a_im[0], s5_log_dt[0], s5_b_re[0], s5_b_im[0])
    sp = (s5_norm_g[0], s5_w_in[0].astype(BF16), abr, abi, bbr, bbi, s5_c_re[0], s5_c_im[0], s5_d[0])
    zeros_p = jnp.zeros((nbp, SSM_COLS), F32)
    z_p, srp, sip = _s5_core(hp.reshape(nbp, seq, d), zeros_p, zeros_p, *sp, tt=S5_TIME_TILE)
    z_s, srs, sis = _s5_core(hs.reshape(nbs, t_new, d), state_s5_re[0].reshape(nbs, SSM_COLS),
                             state_s5_im[0].reshape(nbs, SSM_COLS), *sp, tt=t_new)
    w_glu = s5_w_glu[0].astype(BF16)
    hp = _glu_residual(hp, z_p, w_glu)
    hs = _glu_residual(hs, z_s, w_glu)

    mw = (moe_norm_g[0], moe_w_router[0], moe_b_router[0], moe_w_gate[0].astype(BF16),
          moe_w_up[0].astype(BF16), moe_w_down[0].astype(BF16))
    hp = _moe_layer(hp, *mw)
    hs = _moe_layer(hs, *mw)

    fm_kv = lambda a: jnp.transpose(a.reshape(nbp, N_KV_HEADS, HEAD_DIM, seq), (0, 3, 1, 2))[None]
    st = lambda a, n: a.reshape(1, n, SSM_GROUPS, SSM_STATE)
    return (hp.reshape(nbp, seq, d), hs.reshape(nbs, t_new, d),
            fm_kv(kt_p), fm_kv(vt_p), jnp.transpose(kit_p, (0, 2, 1))[None],
            st(srp, nbp), st(sip, nbp),
            k_s.reshape(1, nbs, t_new, N_KV_HEADS, HEAD_DIM), v_s.reshape(1, nbs, t_new, N_KV_HEADS, HEAD_DIM),
            ki_s.reshape(1, nbs, t_new, IDX_DIM),
            st(srs, nbs), st(sis, nbs))
```

```python
import functools
import math

import jax
import jax.numpy as jnp
import numpy as np
from jax import lax
from jax.experimental import pallas as pl
from jax.experimental.pallas import tpu as pltpu

F32 = jnp.float32
BF16 = jnp.bfloat16
I32 = jnp.int32

D_MODEL = 1024
N_HEADS = 16
N_KV_HEADS = 4
HEAD_DIM = 64
Q_PER_KV = N_HEADS // N_KV_HEADS
IDX_HEADS = 16
IDX_DIM = 64
TOPK_MAX = 256
PAGE_SIZE = 128
ATTN_SCALE = HEAD_DIM ** -0.5
INDEX_SCALE = (IDX_HEADS ** -0.5) * (IDX_DIM ** -0.5)
Q_COLS = N_HEADS * HEAD_DIM
KV_COLS = N_KV_HEADS * HEAD_DIM
QI_COLS = IDX_HEADS * IDX_DIM
SSM_GROUP = 16
SSM_GROUPS = D_MODEL // SSM_GROUP
SSM_STATE = 64
SSM_COLS = SSM_GROUPS * SSM_STATE
N_EXPERTS = 8
RMS_EPS = 1e-6

LANES = 128
SUBLANES_BF16 = 16
MXU_DIM = 256
VMEM_LIMIT = 56 * 1024 * 1024
INT_MIN = -(2 ** 31)
NEG_INF_KEY = INT_MIN + 0x7FFFFF

ROW_TILE = 512
ATTN_Q_TILE = 128
ATTN_KEY_BUCKET = 512
ATTN_WIDE_SEARCH_KEYS = 512
S5_TIME_TILE = 32
S5_SCAN_ELEMS = 8192
MOE_ROW_TILE = 256
MOE_DMA_CHUNK = 256


def _cparams(*sem):
    return pltpu.CompilerParams(dimension_semantics=sem, vmem_limit_bytes=VMEM_LIMIT)


def _rms(x, g):
    return x * lax.rsqrt(jnp.mean(x * x, axis=-1, keepdims=True) + RMS_EPS) * g


def _dot(a, b):
    return jnp.dot(a, b, preferred_element_type=F32)


def _dot_nt(a, b):
    return lax.dot_general(a, b, (((1,), (1,)), ((), ())), preferred_element_type=F32)


def _full(shape):
    n = len(shape)
    return pl.BlockSpec(shape, lambda *_: (0,) * n)


def _attn_proj_kernel(x_ref, ng_ref, wq_ref, wkv_ref, wqi_ref, wkw_ref, gk_ref, gki_ref,
                      q_ref, qi_ref, wi_ref, k_ref, v_ref, ki_ref, *rest, transposed):
    xn = _rms(x_ref[...], ng_ref[...]).astype(BF16)
    q_ref[...] = _dot_nt(xn, wq_ref[...])
    qi_ref[...] = _dot_nt(xn, wqi_ref[...]).astype(BF16)
    if transposed:
        kb_ref, vb_ref, kib_ref = rest
        tm = xn.shape[0]
        kv = _dot_nt(wkv_ref[...], xn)
        k = kv[:KV_COLS].reshape(N_KV_HEADS, HEAD_DIM, tm)
        k = k * lax.rsqrt(jnp.mean(k * k, axis=1, keepdims=True) + RMS_EPS) * gk_ref[...][None]
        k = k.reshape(KV_COLS, tm)
        v = kv[KV_COLS:]
        kw = _dot_nt(wkw_ref[...], xn)
        ki = kw[:IDX_DIM]
        ki = ki * lax.rsqrt(jnp.mean(ki * ki, axis=0, keepdims=True) + RMS_EPS) * gki_ref[...]
        wi_ref[...] = kw[IDX_DIM:IDX_DIM + IDX_HEADS].T
        k_ref[0] = k
        v_ref[0] = v
        ki_ref[0] = ki
        kb_ref[0] = k.astype(BF16)
        vb_ref[0] = v.astype(BF16)
        kib_ref[0] = ki.astype(BF16)
    else:
        kv = _dot_nt(xn, wkv_ref[...])
        gk = gk_ref[...]
        ks = []
        for h in range(N_KV_HEADS):
            seg = kv[:, h * HEAD_DIM:(h + 1) * HEAD_DIM]
            ks.append(_rms(seg, gk))
        k_ref[...] = jnp.concatenate(ks, axis=-1)
        v_ref[...] = kv[:, KV_COLS:]
        kw = _dot_nt(xn, wkw_ref[...])
        ki_ref[...] = _rms(kw[:, :IDX_DIM], gki_ref[...])
        wi_ref[...] = kw[:, IDX_DIM:IDX_DIM + IDX_HEADS]


def _attn_proj(x2d, norm_g, w_parts, g_k, g_kidx, *, seq_len=None):
    m = x2d.shape[0]
    tm = min(ROW_TILE, m)
    wq, wkv, wqi, wkw = w_parts
    transposed = seq_len is not None
    row = lambda cols: pl.BlockSpec((tm, cols), lambda i: (i, 0))
    in_specs = [row(D_MODEL), _full((1, D_MODEL)), _full(wq.shape), _full(wkv.shape), _full(wqi.shape),
                _full(wkw.shape)]
    out_shape = [jax.ShapeDtypeStruct((m, Q_COLS), F32), jax.ShapeDtypeStruct((m, QI_COLS), BF16),
                 jax.ShapeDtypeStruct((m, IDX_HEADS), F32)]
    out_specs = [row(Q_COLS), row(QI_COLS), row(IDX_HEADS)]
    if transposed:
        nb, per = m // seq_len, seq_len // tm
        fm = lambda rows: pl.BlockSpec((1, rows, tm), lambda i: (i // per, 0, i % per))
        in_specs += [_full((HEAD_DIM, 1)), _full((IDX_DIM, 1))]
        gk_in, gki_in = g_k.reshape(HEAD_DIM, 1), g_kidx.reshape(IDX_DIM, 1)
        for dt in (F32, BF16):
            out_shape += [jax.ShapeDtypeStruct((nb, KV_COLS, seq_len), dt),
                          jax.ShapeDtypeStruct((nb, KV_COLS, seq_len), dt),
                          jax.ShapeDtypeStruct((nb, IDX_DIM, seq_len), dt)]
            out_specs += [fm(KV_COLS), fm(KV_COLS), fm(IDX_DIM)]
    else:
        in_specs += [_full((1, HEAD_DIM)), _full((1, IDX_DIM))]
        gk_in, gki_in = g_k.reshape(1, HEAD_DIM), g_kidx.reshape(1, IDX_DIM)
        out_shape += [jax.ShapeDtypeStruct((m, KV_COLS), F32), jax.ShapeDtypeStruct((m, KV_COLS), F32),
                      jax.ShapeDtypeStruct((m, IDX_DIM), F32)]
        out_specs += [row(KV_COLS), row(KV_COLS), row(IDX_DIM)]
    return pl.pallas_call(
        functools.partial(_attn_proj_kernel, transposed=transposed),
        grid=(m // tm,), in_specs=in_specs, out_specs=out_specs, out_shape=out_shape,
        compiler_params=_cparams("parallel"), name="attn_proj",
    )(x2d, norm_g.reshape(1, D_MODEL), wq, wkv, wqi, wkw, gk_in, gki_in)


def _key_to_float(key):
    bits = key ^ (lax.shift_right_arithmetic(key, 31) & 0x7FFFFFFF)
    return pltpu.bitcast(bits, F32)


def _count(mask):
    return jnp.sum(jnp.where(mask, 1.0, 0.0), axis=-1, keepdims=True)


def _topk_select(score, pos, k, lim_ref, pos_bits, bits_per_step=1):
    r, s = score.shape
    kf = float(k)

    def count_ge(key):
        return jnp.where(key < NEG_INF_KEY, float(s), _count(score >= _key_to_float(key)))

    thr = jnp.where(_count(score >= 0.0) >= kf, 0, INT_MIN).astype(I32)
    hi = 30
    while hi >= 0:
        nbits = min(bits_per_step, hi + 1)
        lo = hi - nbits + 1
        passed = [jnp.where(count_ge(thr | (j << lo)) >= kf, 1, 0) for j in range(1, 2 ** nbits)]
        thr = thr | (sum(passed).astype(I32) << lo)
        hi = lo - 1
    thr_f = _key_to_float(thr)
    above = score > thr_f
    tied = score == thr_f
    take = kf - _count(above)
    lim_ref[...] = jnp.full((r, 1), 2 ** pos_bits, I32)
    excess = (_count(tied) > take) & (thr > NEG_INF_KEY)

    @pl.when(jnp.max(jnp.where(excess, 1.0, 0.0)) > 0.0)
    def _():
        def pos_bit(i, lo):
            cand = lo | lax.shift_left(jnp.int32(1), pos_bits - 1 - i)
            return jnp.where(_count(tied & (pos < cand)) < take, cand, lo)

        lim_ref[...] = lax.fori_loop(0, pos_bits, pos_bit, jnp.zeros((r, 1), I32))

    return above | (tied & (pos <= lim_ref[...]))


def _norm_q_heads(q, gq):
    scale = ATTN_SCALE * math.log2(math.e)
    return [_rms(q[:, h * HEAD_DIM:(h + 1) * HEAD_DIM], gq) * scale for h in range(N_HEADS)]


def _attn_prompt_kernel(q_ref, qi_ref, wi_ref, kt_ref, vt_ref, kit_ref, gq_ref, o_ref, lim_ref, *, tq, ksel):
    j = pl.program_id(0)
    seq = kt_ref.shape[2]
    n_buckets = seq // ATTN_KEY_BUCKET
    per_bucket = ATTN_KEY_BUCKET // tq

    def block(s):
        q_pos = j * tq + lax.broadcasted_iota(I32, (tq, 1), 0)
        key_pos = lax.broadcasted_iota(I32, (1, s), 1)
        causal = key_pos <= q_pos
        qi = qi_ref[...]
        wi = wi_ref[...]
        qi_rows = jnp.concatenate([qi[:, h * IDX_DIM:(h + 1) * IDX_DIM] for h in range(IDX_HEADS)], axis=0)
        tiles = []
        for c0 in range(0, s, MXU_DIM):
            sh = _dot(qi_rows, kit_ref[0, :, c0:c0 + MXU_DIM])
            acc = wi[:, 0:1] * jnp.maximum(sh[:tq], 0.0)
            for h in range(1, IDX_HEADS):
                acc = acc + wi[:, h:h + 1] * jnp.maximum(sh[h * tq:(h + 1) * tq], 0.0)
            tiles.append(acc)
        score = jnp.where(causal, jnp.concatenate(tiles, axis=1) * INDEX_SCALE, -jnp.inf)

        heads = _norm_q_heads(q_ref[...], gq_ref[...])
        raw = []
        for n in range(N_KV_HEADS):
            qn = jnp.concatenate(heads[n * Q_PER_KV:(n + 1) * Q_PER_KV], axis=0).astype(BF16)
            raw.append(_dot(qn, kt_ref[0, n * HEAD_DIM:(n + 1) * HEAD_DIM, :s]))

        sel = _topk_select(score, key_pos, ksel, lim_ref, pos_bits=int(math.log2(seq)),
                           bits_per_step=2 if s <= ATTN_WIDE_SEARCH_KEYS else 1)
        bias = jnp.where(sel & causal, 0.0, -jnp.inf)[None]

        outs = [None] * N_HEADS
        ones = jnp.ones((SUBLANES_BF16, s), BF16)
        for n in range(N_KV_HEADS):
            vn = jnp.concatenate([vt_ref[0, n * HEAD_DIM:(n + 1) * HEAD_DIM, :s], ones], axis=0)
            logits = raw[n].reshape(Q_PER_KV, tq, s) + bias
            m = jnp.max(logits, axis=-1, keepdims=True)
            p = jnp.exp2(logits - m)
            pv = _dot_nt(p.reshape(Q_PER_KV * tq, s).astype(BF16), vn)
            pv = pv[:, :HEAD_DIM] / pv[:, HEAD_DIM:HEAD_DIM + 1]
            for g in range(Q_PER_KV):
                outs[n * Q_PER_KV + g] = pv[g * tq:(g + 1) * tq]
        o_ref[...] = jnp.concatenate(outs, axis=-1).astype(BF16)

    for e in range(n_buckets):
        pl.when(j // per_bucket == e)(functools.partial(block, (e + 1) * ATTN_KEY_BUCKET))


def _attn_prompt(q, qi, wi, ktb, vtb, kitb, g_q):
    nb, _, seq = ktb.shape
    tq = ATTN_Q_TILE
    nq = seq // tq
    ksel = min(TOPK_MAX, seq // 4)
    row = lambda cols: pl.BlockSpec((tq, cols), lambda j, b: (b * nq + j, 0))
    fm = lambda rows: pl.BlockSpec((1, rows, seq), lambda j, b: (b, 0, 0))
    return pl.pallas_call(
        functools.partial(_attn_prompt_kernel, tq=tq, ksel=ksel),
        grid=(nq, nb),
        in_specs=[row(Q_COLS), row(QI_COLS), row(IDX_HEADS), fm(KV_COLS), fm(KV_COLS), fm(IDX_DIM),
                  _full((1, HEAD_DIM))],
        out_specs=row(Q_COLS),
        out_shape=jax.ShapeDtypeStruct((nb * seq, Q_COLS), BF16),
        scratch_shapes=[pltpu.VMEM((tq, 1), I32)],
        compiler_params=_cparams("arbitrary", "parallel"), name="attn_prompt",
    )(q, qi, wi, ktb, vtb, kitb, g_q.reshape(1, HEAD_DIM))


def _attn_sample_kernel(pt_ref, q_ref, qi_ref, wi_ref, kn_ref, vn_ref, kin_ref, gq_ref,
                        ck_hbm, cv_hbm, cki_hbm, o_ref, kbuf, vbuf, kibuf, sems, lim_ref,
                        *, n_pages, t_new, ksel):
    b = pl.program_id(0)
    nb = pl.num_programs(0)
    past = n_pages * PAGE_SIZE
    tp = q_ref.shape[1]

    def page_copies(batch, slot, p):
        page = pt_ref[batch, p]
        cols = pl.ds(p * PAGE_SIZE, PAGE_SIZE)
        return (pltpu.make_async_copy(ck_hbm.at[page], kbuf.at[slot, :, cols], sems.at[0, slot]),
                pltpu.make_async_copy(cv_hbm.at[page], vbuf.at[slot, :, cols], sems.at[1, slot]),
                pltpu.make_async_copy(cki_hbm.at[page], kibuf.at[slot, :, cols], sems.at[2, slot]))

    def fetch(batch, slot):
        def issue(p, c):
            for cp in page_copies(batch, slot, p):
                cp.start()
            return c
        lax.fori_loop(0, n_pages, issue, 0)

    def wait(batch, slot):
        def done(p, c):
            for cp in page_copies(batch, slot, p):
                cp.wait()
            return c
        lax.fori_loop(0, n_pages, done, 0)

    slot = b % 2

    @pl.when(b == 0)
    def _():
        fetch(0, 0)

    @pl.when(b + 1 < nb)
    def _():
        fetch(b + 1, 1 - slot)

    wait(b, slot)

    nrow = N_HEADS * tp
    s_all = past + LANES

    def new_rows(ref):
        x = ref[0]
        return jnp.concatenate([x, jnp.zeros((LANES - tp, x.shape[1]), F32)], axis=0).astype(BF16)

    qi = qi_ref[0]
    wi = wi_ref[0]
    qi_rows = jnp.concatenate([qi[:, h * IDX_DIM:(h + 1) * IDX_DIM] for h in range(IDX_HEADS)],
                              axis=0).astype(BF16)
    wi_rows = jnp.concatenate([wi[:, h:h + 1] for h in range(IDX_HEADS)], axis=0)
    s_idx = jnp.concatenate([_dot(qi_rows, kibuf[slot].astype(BF16)),
                             _dot_nt(qi_rows, new_rows(kin_ref))], axis=1)
    s_idx = wi_rows * jnp.maximum(s_idx, 0.0)
    score = jnp.sum(s_idx.reshape(IDX_HEADS, tp, s_all), axis=0) * INDEX_SCALE
    q_pos = past + lax.broadcasted_iota(I32, (tp, 1), 0)
    key_pos = lax.broadcasted_iota(I32, (1, s_all), 1)
    causal = (key_pos <= q_pos) & (key_pos < past + t_new)
    score = jnp.where(causal, score, -jnp.inf)

    heads = _norm_q_heads(q_ref[0], gq_ref[...])
    zero = jnp.zeros((Q_PER_KV * tp, HEAD_DIM), F32)
    qbd = jnp.concatenate([
        jnp.concatenate([jnp.concatenate(heads[n * Q_PER_KV:(n + 1) * Q_PER_KV], axis=0) if c == n else zero
                         for c in range(N_KV_HEADS)], axis=1)
        for n in range(N_KV_HEADS)], axis=0).astype(BF16)
    logits = jnp.concatenate([_dot(qbd, kbuf[slot].astype(BF16)),
                              _dot_nt(qbd, new_rows(kn_ref))], axis=1)

    sel = _topk_select(score, key_pos, ksel, lim_ref, pos_bits=int(math.ceil(math.log2(s_all))),
                       bits_per_step=3)
    bias = jnp.where(sel & causal, 0.0, -jnp.inf)[None]
    logits = (logits.reshape(N_HEADS, tp, s_all) + bias).reshape(nrow, s_all)
    m = jnp.max(logits, axis=-1, keepdims=True)
    p = jnp.exp2(logits - m)
    l = jnp.sum(p, axis=-1, keepdims=True)
    pb = p.astype(BF16)
    pv = _dot_nt(pb[:, :past], vbuf[slot].astype(BF16)) + _dot(pb[:, past:], new_rows(vn_ref))
    pv = pv / l
    outs = []
    for n in range(N_KV_HEADS):
        for g in range(Q_PER_KV):
            r0 = (n * Q_PER_KV + g) * tp
            outs.append(pv[r0:r0 + tp, n * HEAD_DIM:(n + 1) * HEAD_DIM])
    o_ref[0] = jnp.concatenate(outs, axis=-1).astype(BF16)


def _attn_sample(q, qi, wi, k_new, v_new, ki_new, g_q, page_table, ck, cv, cki, *, t_new):
    nb, tp, _ = q.shape
    n_pages = page_table.shape[1]
    past = n_pages * PAGE_SIZE
    ksel = min(TOPK_MAX, (past + t_new) // 4)
    blk = lambda cols: pl.BlockSpec((1, tp, cols), lambda b, pt: (b, 0, 0))
    grid_spec = pltpu.PrefetchScalarGridSpec(
        num_scalar_prefetch=1, grid=(nb,),
        in_specs=[blk(Q_COLS), blk(QI_COLS), blk(IDX_HEADS), blk(KV_COLS), blk(KV_COLS), blk(IDX_DIM),
                  pl.BlockSpec((1, HEAD_DIM), lambda b, pt: (0, 0)),
                  pl.BlockSpec(memory_space=pl.ANY), pl.BlockSpec(memory_space=pl.ANY),
                  pl.BlockSpec(memory_space=pl.ANY)],
        out_specs=blk(Q_COLS),
        scratch_shapes=[pltpu.VMEM((2, KV_COLS, past), F32), pltpu.VMEM((2, KV_COLS, past), F32),
                        pltpu.VMEM((2, IDX_DIM, past), F32), pltpu.SemaphoreType.DMA((3, 2)),
                        pltpu.VMEM((tp, 1), I32)])
    return pl.pallas_call(
        functools.partial(_attn_sample_kernel, n_pages=n_pages, t_new=t_new, ksel=ksel),
        grid_spec=grid_spec, out_shape=jax.ShapeDtypeStruct((nb, tp, Q_COLS), BF16),
        compiler_params=_cparams("arbitrary"), name="attn_sample",
    )(page_table, q, qi, wi, k_new, v_new, ki_new, g_q.reshape(1, HEAD_DIM), ck, cv, cki)


def _attn_out_ffn_kernel(x_ref, o_ref, wo_ref, ng_ref, wg_ref, wu_ref, wd_ref, out_ref):
    h = x_ref[...] + _dot(o_ref[...], wo_ref[...])
    xn = _rms(h, ng_ref[...]).astype(BF16)
    gate = _dot(xn, wg_ref[...])
    up = _dot(xn, wu_ref[...])
    act = (gate * jax.nn.sigmoid(gate) * up).astype(BF16)
    out_ref[...] = h + _dot(act, wd_ref[...])


def _resident(shape):
    n = len(shape)
    return pl.BlockSpec(shape, lambda *_: (0,) * n, pipeline_mode=pl.Buffered(1))


def _attn_out_ffn(x2d, o, w_o, norm_g, w_gate, w_up, w_down):
    m = x2d.shape[0]
    tm = min(ROW_TILE, m)
    row = lambda cols: pl.BlockSpec((tm, cols), lambda i: (i, 0))
    return pl.pallas_call(
        _attn_out_ffn_kernel, grid=(m // tm,),
        in_specs=[row(D_MODEL), row(Q_COLS), _resident(w_o.shape), _full((1, D_MODEL)),
                  _resident(w_gate.shape), _resident(w_up.shape), _resident(w_down.shape)],
        out_specs=row(D_MODEL), out_shape=jax.ShapeDtypeStruct((m, D_MODEL), F32),
        compiler_params=_cparams("parallel"), name="attn_out_ffn",
    )(x2d, o, w_o, norm_g.reshape(1, D_MODEL), w_gate, w_up, w_down)


def _s5_params_kernel(are_ref, aim_ref, ldt_ref, bre_ref, bim_ref, abr_ref, abi_ref, bbr_ref, bbi_ref):
    dt = jnp.exp(ldt_ref[...])
    lr = are_ref[...] * dt
    li = aim_ref[...] * dt
    mag = jnp.exp(lr)
    abr = mag * jnp.cos(li)
    abi = mag * jnp.sin(li)
    den = lr * lr + li * li
    fr = ((abr - 1.0) * lr + abi * li) / den * dt
    fi = (abi * lr - (abr - 1.0) * li) / den * dt
    abr_ref[...] = abr
    abi_ref[...] = abi
    bre = bre_ref[...]
    bim = bim_ref[...]
    bbr_ref[...] = fr * bre - fi * bim
    bbi_ref[...] = fr * bim + fi * bre


def _s5_params(a_re, a_im, log_dt, b_re, b_im):
    g, p, c = b_re.shape
    bt = lambda b: jnp.transpose(b, (0, 2, 1))
    shp3 = jax.ShapeDtypeStruct((g, 1, p), F32)
    shpb = jax.ShapeDtypeStruct((g, c, p), F32)
    return pl.pallas_call(
        _s5_params_kernel, out_shape=[shp3, shp3, shpb, shpb], name="s5_params",
    )(a_re.reshape(g, 1, p), a_im.reshape(g, 1, p), log_dt.reshape(g, 1, 1), bt(b_re), bt(b_im))


def _block_diag_tiles(w, groups_per_tile):
    g, a, b = w.shape
    eye = jnp.eye(groups_per_tile, dtype=w.dtype)
    w = w.reshape(g // groups_per_tile, groups_per_tile, a, b)
    out = w[:, :, :, None, :] * eye[None, :, None, :, None]
    return out.reshape(g // groups_per_tile, groups_per_tile * a, groups_per_tile * b)


def _s5_core_kernel(x_ref, ng_ref, win_ref, perm_ref, permt_ref, bbr_ref, bbi_ref, cr_ref, ci_ref,
                    abr_ref, abi_ref, d_ref, s0r_ref, s0i_ref, z_ref, sr_ref, si_ref, *x_refs,
                    nb, tt):
    i = pl.program_id(0)
    rows = nb * tt
    gpt = MXU_DIM // SSM_GROUP
    n_kt = D_MODEL // MXU_DIM
    st_cols = gpt * SSM_STATE
    xr_refs, xi_refs = x_refs[:n_kt], x_refs[n_kt:]

    @pl.when(i == 0)
    def _():
        sr_ref[...] = s0r_ref[...]
        si_ref[...] = s0i_ref[...]

    x = x_ref[...].reshape(rows, D_MODEL)
    xn = _rms(x, ng_ref[...]).astype(BF16)
    xn = _dot(perm_ref[...], xn).astype(BF16)
    u = _dot(xn, win_ref[...])
    ub = u.astype(BF16)
    sr0, si0 = sr_ref[...], si_ref[...]

    cw = min(S5_SCAN_ELEMS // nb, st_cols)
    ys, srs, sis = [], [], []
    for kt in range(n_kt):
        xr_ref, xi_ref = xr_refs[kt], xi_refs[kt]
        uk = ub[:, kt * MXU_DIM:(kt + 1) * MXU_DIM]
        xr_ref[...] = _dot(uk, bbr_ref[kt])
        xi_ref[...] = _dot(uk, bbi_ref[kt])
        for c0 in range(0, st_cols, cw):
            g0 = kt * st_cols + c0
            ar = jnp.broadcast_to(abr_ref[:, g0:g0 + cw], (nb, cw))
            ai = jnp.broadcast_to(abi_ref[:, g0:g0 + cw], (nb, cw))
            hr, hi = sr0[:, g0:g0 + cw], si0[:, g0:g0 + cw]
            for t in range(tt):
                r = slice(t * nb, (t + 1) * nb)
                hr, hi = (ar * hr - ai * hi + xr_ref[r, c0:c0 + cw],
                          ar * hi + ai * hr + xi_ref[r, c0:c0 + cw])
                xr_ref[r, c0:c0 + cw] = hr
                xi_ref[r, c0:c0 + cw] = hi
            srs.append(hr)
            sis.append(hi)
        ys.append(_dot(xr_ref[...].astype(BF16), cr_ref[kt]) - _dot(xi_ref[...].astype(BF16), ci_ref[kt]))
    sr_ref[...] = jnp.concatenate(srs, axis=-1)
    si_ref[...] = jnp.concatenate(sis, axis=-1)
    y = jnp.concatenate(ys, axis=-1) + d_ref[...] * u
    z = jax.nn.gelu(y).astype(BF16)
    z_ref[...] = _dot(permt_ref[...], z).astype(BF16).reshape(z_ref.shape)


def _s5_core(x, s0_re, s0_im, norm_g, w_in, abr, abi, bbr, bbi, c_re, c_im, d_skip, *, tt):
    nb, t, _ = x.shape
    rows = nb * tt
    gpt = MXU_DIM // SSM_GROUP
    r = np.arange(rows)
    perm_np = np.zeros((rows, rows), np.float32)
    perm_np[r, (r % nb) * tt + r // nb] = 1.0
    perm = jnp.asarray(perm_np, BF16)
    permt = jnp.asarray(perm_np.T, BF16)
    bbr_t = _block_diag_tiles(bbr, gpt).astype(BF16)
    bbi_t = _block_diag_tiles(bbi, gpt).astype(BF16)
    ct = lambda c: _block_diag_tiles(jnp.transpose(c, (0, 2, 1)), gpt).astype(BF16)
    if tt % 8 == 0:
        x_in, x_spec = x, pl.BlockSpec((nb, tt, D_MODEL), lambda i: (0, i, 0))
    else:
        assert tt == t
        x_in, x_spec = x.reshape(rows, D_MODEL), pl.BlockSpec((rows, D_MODEL), lambda i: (0, 0))
    z_spec = (pl.BlockSpec((nb, tt, D_MODEL), lambda i: (0, i, 0)) if tt % 8 == 0
              else pl.BlockSpec((rows, D_MODEL), lambda i: (0, 0)))
    z_shape = (nb, t, D_MODEL) if tt % 8 == 0 else (rows, D_MODEL)
    st = jax.ShapeDtypeStruct((nb, SSM_COLS), F32)
    z, sr, si = pl.pallas_call(
        functools.partial(_s5_core_kernel, nb=nb, tt=tt), grid=(t // tt,),
        in_specs=[x_spec, _full((1, D_MODEL)), _full(w_in.shape), _full(perm.shape), _full(permt.shape),
                  _full(bbr_t.shape), _full(bbi_t.shape), _full((D_MODEL // MXU_DIM, gpt * SSM_STATE, MXU_DIM)),
                  _full((D_MODEL // MXU_DIM, gpt * SSM_STATE, MXU_DIM)),
                  _full((1, SSM_COLS)), _full((1, SSM_COLS)), _full((1, D_MODEL)),
                  _full((nb, SSM_COLS)), _full((nb, SSM_COLS))],
        out_specs=[z_spec, _full((nb, SSM_COLS)), _full((nb, SSM_COLS))],
        out_shape=[jax.ShapeDtypeStruct(z_shape, BF16), st, st],
        scratch_shapes=[pltpu.VMEM((rows, gpt * SSM_STATE), F32)] * (2 * (D_MODEL // MXU_DIM)),
        compiler_params=_cparams("arbitrary"), name="s5_core",
    )(x_in, norm_g.reshape(1, D_MODEL), w_in, perm, permt, bbr_t, bbi_t, ct(c_re), ct(c_im),
      abr.reshape(1, SSM_COLS), abi.reshape(1, SSM_COLS), d_skip.reshape(1, D_MODEL), s0_re, s0_im)
    return z.reshape(nb * t, D_MODEL), sr, si


def _glu_kernel(h_ref, z_ref, w_ref, out_ref):
    gl = _dot(z_ref[...], w_ref[...])
    out_ref[...] = h_ref[...] + gl[:, :D_MODEL] * jax.nn.sigmoid(gl[:, D_MODEL:])


def _glu_residual(h2d, z, w_glu):
    m = h2d.shape[0]
    tm = min(ROW_TILE, m)
    row = pl.BlockSpec((tm, D_MODEL), lambda i: (i, 0))
    return pl.pallas_call(
        _glu_kernel, grid=(m // tm,), in_specs=[row, row, _full(w_glu.shape)], out_specs=row,
        out_shape=jax.ShapeDtypeStruct((m, D_MODEL), F32), compiler_params=_cparams("parallel"), name="s5_glu",
    )(h2d, z, w_glu)


def _moe_router_kernel(h_ref, ng_ref, wr_ref, br_ref, xn_ref, idx_ref, gate_ref, rank_ref, cnt_ref, carry_ref):
    i = pl.program_id(0)
    tm = h_ref.shape[0]

    @pl.when(i == 0)
    def _():
        carry_ref[...] = jnp.zeros_like(carry_ref)

    xn = _rms(h_ref[...], ng_ref[...])
    xn_ref[...] = xn
    logits = lax.dot_general(wr_ref[...], xn, (((1,), (1,)), ((), ())), preferred_element_type=F32,
                             precision=lax.Precision.HIGHEST) + br_ref[...]
    eid = lax.broadcasted_iota(I32, (N_EXPERTS, tm), 0).astype(F32)
    m1 = jnp.max(logits, axis=0, keepdims=True)
    i1 = jnp.min(jnp.where(logits == m1, eid, float(N_EXPERTS)), axis=0, keepdims=True)
    rest = jnp.where(eid == i1, -jnp.inf, logits)
    m2 = jnp.max(rest, axis=0, keepdims=True)
    i2 = jnp.min(jnp.where(rest == m2, eid, float(N_EXPERTS)), axis=0, keepdims=True)
    e2 = jnp.exp(m2 - m1)
    den = 1.0 + e2
    idx_ref[...] = jnp.concatenate([i1, i2], axis=0).astype(I32)
    gate_ref[...] = jnp.concatenate([1.0 / den, e2 / den], axis=0)

    onehot = jnp.where((eid == i1) | (eid == i2), 1.0, 0.0)
    src = lax.broadcasted_iota(I32, (tm, tm), 0)
    dst = lax.broadcasted_iota(I32, (tm, tm), 1)
    tri = jnp.where(src <= dst, 1.0, 0.0).astype(BF16)
    cum = _dot(onehot.astype(BF16), tri) + carry_ref[:, 0:1]
    before = cum - 1.0
    r1 = jnp.sum(jnp.where(eid == i1, before, 0.0), axis=0, keepdims=True)
    r2 = jnp.sum(jnp.where(eid == i2, before, 0.0), axis=0, keepdims=True)
    rank_ref[...] = jnp.concatenate([r1, r2], axis=0).astype(I32)
    total = cum[:, tm - 1:tm]
    carry_ref[...] = jnp.broadcast_to(total, carry_ref.shape)
    cnt_ref[...] = jnp.broadcast_to(total, cnt_ref.shape).astype(I32)


def _moe_router(h2d, norm_g, w_router, b_router):
    m = h2d.shape[0]
    tm = min(ROW_TILE, m)
    row = pl.BlockSpec((tm, D_MODEL), lambda i: (i, 0))
    col2 = pl.BlockSpec((2, tm), lambda i: (0, i))
    return pl.pallas_call(
        _moe_router_kernel, grid=(m // tm,),
        in_specs=[row, _full((1, D_MODEL)), _full((N_EXPERTS, D_MODEL)), _full((N_EXPERTS, 1))],
        out_specs=[row, col2, col2, col2, _full((N_EXPERTS, LANES))],
        out_shape=[jax.ShapeDtypeStruct((m, D_MODEL), F32), jax.ShapeDtypeStruct((2, m), I32),
                   jax.ShapeDtypeStruct((2, m), F32), jax.ShapeDtypeStruct((2, m), I32),
                   jax.ShapeDtypeStruct((N_EXPERTS, LANES), I32)],
        scratch_shapes=[pltpu.VMEM((N_EXPERTS, LANES), F32)],
        compiler_params=_cparams("arbitrary"), name="moe_router",
    )(h2d, norm_g.reshape(1, D_MODEL), w_router.T, b_router.reshape(N_EXPERTS, 1))


def _moe_scatter_kernel(tend_ref, pos_ref, xn_ref, xs_hbm, zbuf, sem, zsem, *, n_tiles):
    i = pl.program_id(0)
    ch = pos_ref.shape[1]
    tg = zbuf.shape[0]

    @pl.when(i == 0)
    def _():
        zbuf[...] = jnp.zeros_like(zbuf)
        used = tend_ref[N_EXPERTS - 1]

        def clears():
            for e in range(N_EXPERTS):
                first = tend_ref[e - 1] if e else 0
                yield tend_ref[e] > first, tend_ref[e] - 1
            for r in range(N_EXPERTS):
                yield used + r < n_tiles, used + r

        for go, tile in clears():
            @pl.when(go)
            def _():
                pltpu.make_async_copy(zbuf, xs_hbm.at[pl.ds(tile * tg, tg)], zsem).start()
        for go, tile in clears():
            @pl.when(go)
            def _():
                pltpu.make_async_copy(zbuf, xs_hbm.at[pl.ds(tile * tg, tg)], zsem).wait()

    def copy(t, k):
        return pltpu.make_async_copy(xn_ref.at[pl.ds(t, 1)], xs_hbm.at[pl.ds(pos_ref[k, t], 1)], sem)

    def issue(t, c):
        copy(t, 0).start()
        copy(t, 1).start()
        return c

    def drain(t, c):
        copy(t, 0).wait()
        copy(t, 1).wait()
        return c

    lax.fori_loop(0, ch, issue, 0, unroll=8)
    lax.fori_loop(0, ch, drain, 0, unroll=8)


def _moe_scatter(xn, pos, tile_end, n_tiles):
    m = xn.shape[0]
    ch = min(MOE_DMA_CHUNK, m)
    tg = MOE_ROW_TILE
    grid_spec = pltpu.PrefetchScalarGridSpec(
        num_scalar_prefetch=1, grid=(m // ch,),
        in_specs=[pl.BlockSpec((2, ch), lambda i, te: (0, i), memory_space=pltpu.SMEM),
                  pl.BlockSpec((ch, D_MODEL), lambda i, te: (i, 0))],
        out_specs=pl.BlockSpec(memory_space=pl.ANY),
        scratch_shapes=[pltpu.VMEM((tg, D_MODEL), F32), pltpu.SemaphoreType.DMA(()),
                        pltpu.SemaphoreType.DMA(())])
    return pl.pallas_call(
        functools.partial(_moe_scatter_kernel, n_tiles=n_tiles), grid_spec=grid_spec,
        out_shape=jax.ShapeDtypeStruct((n_tiles * tg, D_MODEL), F32),
        compiler_params=_cparams("arbitrary"), name="moe_scatter",
    )(tile_end, pos, xn)


def _moe_expert_kernel(te_ref, nt_ref, xs_ref, wg_ref, wu_ref, wd_ref, ys_ref):
    i = pl.program_id(0)

    @pl.when(i < nt_ref[0])
    def _():
        x = xs_ref[...].astype(BF16)
        gate = _dot(x, wg_ref[0])
        up = _dot(x, wu_ref[0])
        act = (gate * jax.nn.sigmoid(gate) * up).astype(BF16)
        ys_ref[...] = _dot(act, wd_ref[0])

    @pl.when(i >= nt_ref[0])
    def _():
        ys_ref[...] = jnp.zeros_like(ys_ref)


def _moe_experts(xs, tile_expert, n_tiles_used, w_gate, w_up, w_down):
    n_rows = xs.shape[0]
    tg = MOE_ROW_TILE
    d_ff = w_gate.shape[2]
    row = pl.BlockSpec((tg, D_MODEL), lambda i, te, nt: (i, 0))
    grid_spec = pltpu.PrefetchScalarGridSpec(
        num_scalar_prefetch=2, grid=(n_rows // tg,),
        in_specs=[row,
                  pl.BlockSpec((1, D_MODEL, d_ff), lambda i, te, nt: (te[i], 0, 0)),
                  pl.BlockSpec((1, D_MODEL, d_ff), lambda i, te, nt: (te[i], 0, 0)),
                  pl.BlockSpec((1, d_ff, D_MODEL), lambda i, te, nt: (te[i], 0, 0))],
        out_specs=row)
    return pl.pallas_call(
        _moe_expert_kernel, grid_spec=grid_spec, out_shape=jax.ShapeDtypeStruct((n_rows, D_MODEL), F32),
        compiler_params=_cparams("arbitrary"), name="moe_experts",
    )(tile_expert, n_tiles_used, xs, w_gate, w_up, w_down)


def _moe_combine_kernel(pos_ref, pos_next_ref, h_ref, gate_ref, ys_hbm, out_ref, buf, sems):
    i = pl.program_id(0)
    n = pl.num_programs(0)
    tm = h_ref.shape[0]
    slot = i % 2

    def copy(p_ref, s, t, k):
        return pltpu.make_async_copy(ys_hbm.at[pl.ds(p_ref[k, t], 1)], buf.at[s, k, pl.ds(t, 1)], sems.at[s])

    def issue(p_ref, s):
        def body(t, c):
            copy(p_ref, s, t, 0).start()
            copy(p_ref, s, t, 1).start()
            return c
        lax.fori_loop(0, tm, body, 0, unroll=8)

    @pl.when(i == 0)
    def _():
        issue(pos_ref, 0)

    @pl.when(i + 1 < n)
    def _():
        issue(pos_next_ref, 1 - slot)

    def drain(t, c):
        copy(pos_ref, slot, t, 0).wait()
        copy(pos_ref, slot, t, 1).wait()
        return c

    lax.fori_loop(0, tm, drain, 0, unroll=8)
    gate = gate_ref[...]
    out_ref[...] = h_ref[...] + gate[:, 0:1] * buf[slot, 0] + gate[:, 1:2] * buf[slot, 1]


def _moe_combine(h2d, gates_t, pos, ys):
    m = h2d.shape[0]
    tm = min(MOE_DMA_CHUNK, m)
    n = m // tm
    row = pl.BlockSpec((tm, D_MODEL), lambda i: (i, 0))
    return pl.pallas_call(
        _moe_combine_kernel, grid=(n,),
        in_specs=[pl.BlockSpec((2, tm), lambda i: (0, i), memory_space=pltpu.SMEM),
                  pl.BlockSpec((2, tm), lambda i: (0, jnp.minimum(i + 1, n - 1)), memory_space=pltpu.SMEM),
                  row, pl.BlockSpec((tm, 2), lambda i: (i, 0)), pl.BlockSpec(memory_space=pl.ANY)],
        out_specs=row, out_shape=jax.ShapeDtypeStruct((m, D_MODEL), F32),
        scratch_shapes=[pltpu.VMEM((2, 2, tm, D_MODEL), F32), pltpu.SemaphoreType.DMA((2,))],
        compiler_params=_cparams("arbitrary"), name="moe_combine",
    )(pos, pos, h2d, gates_t, ys)


def _moe_layer(h2d, norm_g, w_router, b_router, w_gate, w_up, w_down):
    m = h2d.shape[0]
    tg = MOE_ROW_TILE
    xn, idx, gate, rank, cnt = _moe_router(h2d, norm_g, w_router, b_router)
    counts = cnt[:, 0]
    tiles = (counts + tg - 1) // tg
    tile_end = jnp.cumsum(tiles).astype(I32)
    base = (tile_end - tiles) * tg
    pos = rank
    for e in range(N_EXPERTS):
        pos = pos + jnp.where(idx == e, base[e], 0)
    n_tiles = (2 * m) // tg + N_EXPERTS
    tile_expert = jnp.sum((jnp.arange(n_tiles, dtype=I32)[:, None] >= tile_end[None, :]).astype(I32), axis=1)
    tile_expert = jnp.minimum(tile_expert, N_EXPERTS - 1)
    xs = _moe_scatter(xn, pos, tile_end, n_tiles)
    ys = _moe_experts(xs, tile_expert, tile_end[-1:], w_gate, w_up, w_down)
    return _moe_combine(h2d, gate.T, pos, ys)


def kernel(x_prompt, x_sample, cache_k, cache_v, cache_kidx, state_s5_re, state_s5_im, page_table,
           attn_norm_g, attn_w_in, attn_g_q, attn_g_k, attn_g_kidx, attn_w_o,
           dense_norm_g, dense_w_gate, dense_w_up, dense_w_down,
           s5_norm_g, s5_w_in, s5_a_re, s5_a_im, s5_log_dt, s5_b_re, s5_b_im, s5_c_re, s5_c_im,
           s5_d, s5_w_glu,
           moe_norm_g, moe_w_router, moe_b_router, moe_w_gate, moe_w_up, moe_w_down):
    nbp, seq, d = x_prompt.shape
    nbs, t_new, _ = x_sample.shape
    mp, ms = nbp * seq, nbs * t_new
    tp = 8
    xp2 = x_prompt.reshape(mp, d)
    xs2 = x_sample.reshape(ms, d)

    w_in_t = attn_w_in[0].T.astype(BF16)
    o1, o2, o3, o4, o5 = np.cumsum([Q_COLS, KV_COLS, KV_COLS, QI_COLS, IDX_DIM]).tolist()
    w_kw = jnp.pad(w_in_t[o4:], ((0, LANES - (w_in_t.shape[0] - o4)), (0, 0)))
    w_parts = (w_in_t[:o1], w_in_t[o1:o3], w_in_t[o3:o4], w_kw)
    ap = (attn_norm_g[0], w_parts, attn_g_k[0], attn_g_kidx[0])

    q_p, qi_p, wi_p, kt_p, vt_p, kit_p, ktb, vtb, kitb = _attn_proj(xp2, *ap, seq_len=seq)
    o_p = _attn_prompt(q_p, qi_p, wi_p, ktb, vtb, kitb, attn_g_q[0])

    q_s, qi_s, wi_s, k_s, v_s, ki_s = _attn_proj(xs2, *ap)
    pad_t = lambda a: jnp.pad(a.reshape(nbs, t_new, a.shape[-1]), ((0, 0), (0, tp - t_new), (0, 0)))
    n_pool = cache_k.shape[1]
    ck = jnp.transpose(cache_k[0], (0, 2, 3, 1)).reshape(n_pool, KV_COLS, PAGE_SIZE)
    cv = jnp.transpose(cache_v[0], (0, 2, 3, 1)).reshape(n_pool, KV_COLS, PAGE_SIZE)
    cki = jnp.transpose(cache_kidx[0], (0, 2, 1))
    o_s = _attn_sample(pad_t(q_s), pad_t(qi_s.astype(F32)), pad_t(wi_s), pad_t(k_s), pad_t(v_s), pad_t(ki_s),
                       attn_g_q[0], page_table, ck, cv, cki, t_new=t_new)
    o_s = o_s[:, :t_new].reshape(ms, Q_COLS)

    fp = (attn_w_o[0].astype(BF16), dense_norm_g[0], dense_w_gate[0].astype(BF16),
          dense_w_up[0].astype(BF16), dense_w_down[0].astype(BF16))
    hp = _attn_out_ffn(xp2, o_p, *fp)
    hs = _attn_out_ffn(xs2, o_s, *fp)

    abr, abi, bbr, bbi = _s5_params(s5_a_re[0], s5_a_im[0], s5_log_dt[0], s5_b_re[0], s5_b_im[0])
    sp = (s5_norm_g[0], s5_w_in[0].astype(BF16), abr, abi, bbr, bbi, s5_c_re[0], s5_c_im[0], s5_d[0])
    zeros_p = jnp.zeros((nbp, SSM_COLS), F32)
    z_p, srp, sip = _s5_core(hp.reshape(nbp, seq, d), zeros_p, zeros_p, *sp, tt=S5_TIME_TILE)
    z_s, srs, sis = _s5_core(hs.reshape(nbs, t_new, d), state_s5_re[0].reshape(nbs, SSM_COLS),
                             state_s5_im[0].reshape(nbs, SSM_COLS), *sp, tt=t_new)
    w_glu = s5_w_glu[0].astype(BF16)
    hp = _glu_residual(hp, z_p, w_glu)
    hs = _glu_residual(hs, z_s, w_glu)

    mw = (moe_norm_g[0], moe_w_router[0], moe_b_router[0], moe_w_gate[0].astype(BF16),
          moe_w_up[0].astype(BF16), moe_w_down[0].astype(BF16))
    hp = _moe_layer(hp, *mw)
    hs = _moe_layer(hs, *mw)

    fm_kv = lambda a: jnp.transpose(a.reshape(nbp, N_KV_HEADS, HEAD_DIM, seq), (0, 3, 1, 2))[None]
    st = lambda a, n: a.reshape(1, n, SSM_GROUPS, SSM_STATE)
    return (hp.reshape(nbp, seq, d), hs.reshape(nbs, t_new, d),
            fm_kv(kt_p), fm_kv(vt_p), jnp.transpose(kit_p, (0, 2, 1))[None],
            st(srp, nbp), st(sip, nbp),
            k_s.reshape(1, nbs, t_new, N_KV_HEADS, HEAD_DIM), v_s.reshape(1, nbs, t_new, N_KV_HEADS, HEAD_DIM),
            ki_s.reshape(1, nbs, t_new, IDX_DIM),
            st(srs, nbs), st(sis, nbs))
```

```python
import functools
import math

import jax
import jax.numpy as jnp
import numpy as np
from jax import lax
from jax.experimental import pallas as pl
from jax.experimental.pallas import tpu as pltpu

F32 = jnp.float32
BF16 = jnp.bfloat16
I32 = jnp.int32

D_MODEL = 1024
N_HEADS = 16
N_KV_HEADS = 4
HEAD_DIM = 64
Q_PER_KV = N_HEADS // N_KV_HEADS
IDX_HEADS = 16
IDX_DIM = 64
TOPK_MAX = 256
PAGE_SIZE = 128
ATTN_SCALE = HEAD_DIM ** -0.5
INDEX_SCALE = (IDX_HEADS ** -0.5) * (IDX_DIM ** -0.5)
Q_COLS = N_HEADS * HEAD_DIM
KV_COLS = N_KV_HEADS * HEAD_DIM
QI_COLS = IDX_HEADS * IDX_DIM
SSM_GROUP = 16
SSM_GROUPS = D_MODEL // SSM_GROUP
SSM_STATE = 64
SSM_COLS = SSM_GROUPS * SSM_STATE
N_EXPERTS = 8
RMS_EPS = 1e-6

LANES = 128
SUBLANES_BF16 = 16
MXU_DIM = 256
VMEM_LIMIT = 56 * 1024 * 1024
INT_MIN = -(2 ** 31)
NEG_INF_KEY = INT_MIN + 0x7FFFFF

ROW_TILE = 512
ATTN_Q_TILE = 128
ATTN_KEY_BUCKET = 512
ATTN_WIDE_SEARCH_KEYS = 512
S5_TIME_TILE = 32
S5_SCAN_ELEMS = 8192
MOE_ROW_TILE = 256
MOE_DMA_CHUNK = 256


def _cparams(*sem):
    return pltpu.CompilerParams(dimension_semantics=sem, vmem_limit_bytes=VMEM_LIMIT)


def _rms(x, g):
    return x * lax.rsqrt(jnp.mean(x * x, axis=-1, keepdims=True) + RMS_EPS) * g


def _dot(a, b):
    return jnp.dot(a, b, preferred_element_type=F32)


def _dot_nt(a, b):
    return lax.dot_general(a, b, (((1,), (1,)), ((), ())), preferred_element_type=F32)


def _full(shape):
    n = len(shape)
    return pl.BlockSpec(shape, lambda *_: (0,) * n)


def _attn_proj_kernel(x_ref, ng_ref, wq_ref, wkv_ref, wqi_ref, wkw_ref, gk_ref, gki_ref,
                      q_ref, qi_ref, wi_ref, k_ref, v_ref, ki_ref, *rest, transposed):
    xn = _rms(x_ref[...], ng_ref[...]).astype(BF16)
    q_ref[...] = _dot_nt(xn, wq_ref[...])
    qi_ref[...] = _dot_nt(xn, wqi_ref[...]).astype(BF16)
    if transposed:
        kb_ref, vb_ref, kib_ref = rest
        tm = xn.shape[0]
        kv = _dot_nt(wkv_ref[...], xn)
        k = kv[:KV_COLS].reshape(N_KV_HEADS, HEAD_DIM, tm)
        k = k * lax.rsqrt(jnp.mean(k * k, axis=1, keepdims=True) + RMS_EPS) * gk_ref[...][None]
        k = k.reshape(KV_COLS, tm)
        v = kv[KV_COLS:]
        kw = _dot_nt(wkw_ref[...], xn)
        ki = kw[:IDX_DIM]
        ki = ki * lax.rsqrt(jnp.mean(ki * ki, axis=0, keepdims=True) + RMS_EPS) * gki_ref[...]
        wi_ref[...] = kw[IDX_DIM:IDX_DIM + IDX_HEADS].T
        k_ref[0] = k
        v_ref[0] = v
        ki_ref[0] = ki
        kb_ref[0] = k.astype(BF16)
        vb_ref[0] = v.astype(BF16)
        kib_ref[0] = ki.astype(BF16)
    else:
        kv = _dot_nt(xn, wkv_ref[...])
        gk = gk_ref[...]
        ks = []
        for h in range(N_KV_HEADS):
            seg = kv[:, h * HEAD_DIM:(h + 1) * HEAD_DIM]
            ks.append(_rms(seg, gk))
        k_ref[...] = jnp.concatenate(ks, axis=-1)
        v_ref[...] = kv[:, KV_COLS:]
        kw = _dot_nt(xn, wkw_ref[...])
        ki_ref[...] = _rms(kw[:, :IDX_DIM], gki_ref[...])
        wi_ref[...] = kw[:, IDX_DIM:IDX_DIM + IDX_HEADS]


def _attn_proj(x2d, norm_g, w_parts, g_k, g_kidx, *, seq_len=None):
    m = x2d.shape[0]
    tm = min(ROW_TILE, m)
    wq, wkv, wqi, wkw = w_parts
    transposed = seq_len is not None
    row = lambda cols: pl.BlockSpec((tm, cols), lambda i: (i, 0))
    in_specs = [row(D_MODEL), _full((1, D_MODEL)), _full(wq.shape), _full(wkv.shape), _full(wqi.shape),
                _full(wkw.shape)]
    out_shape = [jax.ShapeDtypeStruct((m, Q_COLS), F32), jax.ShapeDtypeStruct((m, QI_COLS), BF16),
                 jax.ShapeDtypeStruct((m, IDX_HEADS), F32)]
    out_specs = [row(Q_COLS), row(QI_COLS), row(IDX_HEADS)]
    if transposed:
        nb, per = m // seq_len, seq_len // tm
        fm = lambda rows: pl.BlockSpec((1, rows, tm), lambda i: (i // per, 0, i % per))
        in_specs += [_full((HEAD_DIM, 1)), _full((IDX_DIM, 1))]
        gk_in, gki_in = g_k.reshape(HEAD_DIM, 1), g_kidx.reshape(IDX_DIM, 1)
        for dt in (F32, BF16):
            out_shape += [jax.ShapeDtypeStruct((nb, KV_COLS, seq_len), dt),
                          jax.ShapeDtypeStruct((nb, KV_COLS, seq_len), dt),
                          jax.ShapeDtypeStruct((nb, IDX_DIM, seq_len), dt)]
            out_specs += [fm(KV_COLS), fm(KV_COLS), fm(IDX_DIM)]
    else:
        in_specs += [_full((1, HEAD_DIM)), _full((1, IDX_DIM))]
        gk_in, gki_in = g_k.reshape(1, HEAD_DIM), g_kidx.reshape(1, IDX_DIM)
        out_shape += [jax.ShapeDtypeStruct((m, KV_COLS), F32), jax.ShapeDtypeStruct((m, KV_COLS), F32),
                      jax.ShapeDtypeStruct((m, IDX_DIM), F32)]
        out_specs += [row(KV_COLS), row(KV_COLS), row(IDX_DIM)]
    return pl.pallas_call(
        functools.partial(_attn_proj_kernel, transposed=transposed),
        grid=(m // tm,), in_specs=in_specs, out_specs=out_specs, out_shape=out_shape,
        compiler_params=_cparams("parallel"), name="attn_proj",
    )(x2d, norm_g.reshape(1, D_MODEL), wq, wkv, wqi, wkw, gk_in, gki_in)


def _key_to_float(key):
    bits = key ^ (lax.shift_right_arithmetic(key, 31) & 0x7FFFFFFF)
    return pltpu.bitcast(bits, F32)


def _count(mask):
    return jnp.sum(jnp.where(mask, 1.0, 0.0), axis=-1, keepdims=True)


def _topk_select(score, pos, k, lim_ref, pos_bits, bits_per_step=1):
    r, s = score.shape
    kf = float(k)

    def count_ge(key):
        return jnp.where(key < NEG_INF_KEY, float(s), _count(score >= _key_to_float(key)))

    thr = jnp.where(_count(score >= 0.0) >= kf, 0, INT_MIN).astype(I32)
    hi = 30
    while hi >= 0:
        nbits = min(bits_per_step, hi + 1)
        lo = hi - nbits + 1
        passed = [jnp.where(count_ge(thr | (j << lo)) >= kf, 1, 0) for j in range(1, 2 ** nbits)]
        thr = thr | (sum(passed).astype(I32) << lo)
        hi = lo - 1
    thr_f = _key_to_float(thr)
    above = score > thr_f
    tied = score == thr_f
    take = kf - _count(above)
    lim_ref[...] = jnp.full((r, 1), 2 ** pos_bits, I32)
    excess = (_count(tied) > take) & (thr > NEG_INF_KEY)

    @pl.when(jnp.max(jnp.where(excess, 1.0, 0.0)) > 0.0)
    def _():
        def pos_bit(i, lo):
            cand = lo | lax.shift_left(jnp.int32(1), pos_bits - 1 - i)
            return jnp.where(_count(tied & (pos < cand)) < take, cand, lo)

        lim_ref[...] = lax.fori_loop(0, pos_bits, pos_bit, jnp.zeros((r, 1), I32))

    return above | (tied & (pos <= lim_ref[...]))


def _norm_q_heads(q, gq):
    scale = ATTN_SCALE * math.log2(math.e)
    return [_rms(q[:, h * HEAD_DIM:(h + 1) * HEAD_DIM], gq) * scale for h in range(N_HEADS)]


def _attn_prompt_kernel(q_ref, qi_ref, wi_ref, kt_ref, vt_ref, kit_ref, gq_ref, o_ref, lim_ref, *, tq, ksel):
    j = pl.program_id(0)
    seq = kt_ref.shape[2]
    n_buckets = seq // ATTN_KEY_BUCKET
    per_bucket = ATTN_KEY_BUCKET // tq

    def block(s):
        q_pos = j * tq + lax.broadcasted_iota(I32, (tq, 1), 0)
        key_pos = lax.broadcasted_iota(I32, (1, s), 1)
        causal = key_pos <= q_pos
        qi = qi_ref[...]
        wi = wi_ref[...]
        qi_rows = jnp.concatenate([qi[:, h * IDX_DIM:(h + 1) * IDX_DIM] for h in range(IDX_HEADS)], axis=0)
        tiles = []
        for c0 in range(0, s, MXU_DIM):
            sh = _dot(qi_rows, kit_ref[0, :, c0:c0 + MXU_DIM])
            acc = wi[:, 0:1] * jnp.maximum(sh[:tq], 0.0)
            for h in range(1, IDX_HEADS):
                acc = acc + wi[:, h:h + 1] * jnp.maximum(sh[h * tq:(h + 1) * tq], 0.0)
            tiles.append(acc)
        score = jnp.where(causal, jnp.concatenate(tiles, axis=1) * INDEX_SCALE, -jnp.inf)

        heads = _norm_q_heads(q_ref[...], gq_ref[...])
        raw = []
        for n in range(N_KV_HEADS):
            qn = jnp.concatenate(heads[n * Q_PER_KV:(n + 1) * Q_PER_KV], axis=0).astype(BF16)
            raw.append(_dot(qn, kt_ref[0, n * HEAD_DIM:(n + 1) * HEAD_DIM, :s]))

        sel = _topk_select(score, key_pos, ksel, lim_ref, pos_bits=int(math.log2(seq)),
                           bits_per_step=2 if s <= ATTN_WIDE_SEARCH_KEYS else 1)
        bias = jnp.where(sel & causal, 0.0, -jnp.inf)[None]

        outs = [None] * N_HEADS
        ones = jnp.ones((SUBLANES_BF16, s), BF16)
        for n in range(N_KV_HEADS):
            vn = jnp.concatenate([vt_ref[0, n * HEAD_DIM:(n + 1) * HEAD_DIM, :s], ones], axis=0)
            logits = raw[n].reshape(Q_PER_KV, tq, s) + bias
            m = jnp.max(logits, axis=-1, keepdims=True)
            p = jnp.exp2(logits - m)
            pv = _dot_nt(p.reshape(Q_PER_KV * tq, s).astype(BF16), vn)
            pv = pv[:, :HEAD_DIM] / pv[:, HEAD_DIM:HEAD_DIM + 1]
            for g in range(Q_PER_KV):
                outs[n * Q_PER_KV + g] = pv[g * tq:(g + 1) * tq]
        o_ref[...] = jnp.concatenate(outs, axis=-1).astype(BF16)

    for e in range(n_buckets):
        pl.when(j // per_bucket == e)(functools.partial(block, (e + 1) * ATTN_KEY_BUCKET))


def _attn_prompt(q, qi, wi, ktb, vtb, kitb, g_q):
    nb, _, seq = ktb.shape
    tq = ATTN_Q_TILE
    nq = seq // tq
    ksel = min(TOPK_MAX, seq // 4)
    row = lambda cols: pl.BlockSpec((tq, cols), lambda j, b: (b * nq + j, 0))
    fm = lambda rows: pl.BlockSpec((1, rows, seq), lambda j, b: (b, 0, 0))
    return pl.pallas_call(
        functools.partial(_attn_prompt_kernel, tq=tq, ksel=ksel),
        grid=(nq, nb),
        in_specs=[row(Q_COLS), row(QI_COLS), row(IDX_HEADS), fm(KV_COLS), fm(KV_COLS), fm(IDX_DIM),
                  _full((1, HEAD_DIM))],
        out_specs=row(Q_COLS),
        out_shape=jax.ShapeDtypeStruct((nb * seq, Q_COLS), BF16),
        scratch_shapes=[pltpu.VMEM((tq, 1), I32)],
        compiler_params=_cparams("arbitrary", "arbitrary"), name="attn_prompt",
    )(q, qi, wi, ktb, vtb, kitb, g_q.reshape(1, HEAD_DIM))


def _attn_sample_kernel(pt_ref, q_ref, qi_ref, wi_ref, kn_ref, vn_ref, kin_ref, gq_ref,
                        ck_hbm, cv_hbm, cki_hbm, o_ref, kbuf, vbuf, kibuf, sems, lim_ref,
                        *, n_pages, t_new, ksel):
    b = pl.program_id(0)
    nb = pl.num_programs(0)
    past = n_pages * PAGE_SIZE
    tp = q_ref.shape[1]

    def page_copies(batch, slot, p):
        page = pt_ref[batch, p]
        cols = pl.ds(p * PAGE_SIZE, PAGE_SIZE)
        return (pltpu.make_async_copy(ck_hbm.at[page], kbuf.at[slot, :, cols], sems.at[0, slot]),
                pltpu.make_async_copy(cv_hbm.at[page], vbuf.at[slot, :, cols], sems.at[1, slot]),
                pltpu.make_async_copy(cki_hbm.at[page], kibuf.at[slot, :, cols], sems.at[2, slot]))

    def fetch(batch, slot):
        def issue(p, c):
            for cp in page_copies(batch, slot, p):
                cp.start()
            return c
        lax.fori_loop(0, n_pages, issue, 0)

    def wait(batch, slot):
        def done(p, c):
            for cp in page_copies(batch, slot, p):
                cp.wait()
            return c
        lax.fori_loop(0, n_pages, done, 0)

    slot = b % 2

    @pl.when(b == 0)
    def _():
        fetch(0, 0)

    @pl.when(b + 1 < nb)
    def _():
        fetch(b + 1, 1 - slot)

    wait(b, slot)

    nrow = N_HEADS * tp
    s_all = past + LANES

    def new_rows(ref):
        x = ref[0]
        return jnp.concatenate([x, jnp.zeros((LANES - tp, x.shape[1]), F32)], axis=0).astype(BF16)

    qi = qi_ref[0]
    wi = wi_ref[0]
    qi_rows = jnp.concatenate([qi[:, h * IDX_DIM:(h + 1) * IDX_DIM] for h in range(IDX_HEADS)],
                              axis=0).astype(BF16)
    wi_rows = jnp.concatenate([wi[:, h:h + 1] for h in range(IDX_HEADS)], axis=0)
    s_idx = jnp.concatenate([_dot(qi_rows, kibuf[slot].astype(BF16)),
                             _dot_nt(qi_rows, new_rows(kin_ref))], axis=1)
    s_idx = wi_rows * jnp.maximum(s_idx, 0.0)
    score = jnp.sum(s_idx.reshape(IDX_HEADS, tp, s_all), axis=0) * INDEX_SCALE
    q_pos = past + lax.broadcasted_iota(I32, (tp, 1), 0)
    key_pos = lax.broadcasted_iota(I32, (1, s_all), 1)
    causal = (key_pos <= q_pos) & (key_pos < past + t_new)
    score = jnp.where(causal, score, -jnp.inf)

    heads = _norm_q_heads(q_ref[0], gq_ref[...])
    zero = jnp.zeros((Q_PER_KV * tp, HEAD_DIM), F32)
    qbd = jnp.concatenate([
        jnp.concatenate([jnp.concatenate(heads[n * Q_PER_KV:(n + 1) * Q_PER_KV], axis=0) if c == n else zero
                         for c in range(N_KV_HEADS)], axis=1)
        for n in range(N_KV_HEADS)], axis=0).astype(BF16)
    logits = jnp.concatenate([_dot(qbd, kbuf[slot].astype(BF16)),
                              _dot_nt(qbd, new_rows(kn_ref))], axis=1)

    sel = _topk_select(score, key_pos, ksel, lim_ref, pos_bits=int(math.ceil(math.log2(s_all))),
                       bits_per_step=3)
    bias = jnp.where(sel & causal, 0.0, -jnp.inf)[None]
    logits = (logits.reshape(N_HEADS, tp, s_all) + bias).reshape(nrow, s_all)
    m = jnp.max(logits, axis=-1, keepdims=True)
    p = jnp.exp2(logits - m)
    l = jnp.sum(p, axis=-1, keepdims=True)
    pb = p.astype(BF16)
    pv = _dot_nt(pb[:, :past], vbuf[slot].astype(BF16)) + _dot(pb[:, past:], new_rows(vn_ref))
    pv = pv / l
    outs = []
    for n in range(N_KV_HEADS):
        for g in range(Q_PER_KV):
            r0 = (n * Q_PER_KV + g) * tp
            outs.append(pv[r0:r0 + tp, n * HEAD_DIM:(n + 1) * HEAD_DIM])
    o_ref[0] = jnp.concatenate(outs, axis=-1).astype(BF16)


def _attn_sample(q, qi, wi, k_new, v_new, ki_new, g_q, page_table, ck, cv, cki, *, t_new):
    nb, tp, _ = q.shape
    n_pages = page_table.shape[1]
    past = n_pages * PAGE_SIZE
    ksel = min(TOPK_MAX, (past + t_new) // 4)
    blk = lambda cols: pl.BlockSpec((1, tp, cols), lambda b, pt: (b, 0, 0))
    grid_spec = pltpu.PrefetchScalarGridSpec(
        num_scalar_prefetch=1, grid=(nb,),
        in_specs=[blk(Q_COLS), blk(QI_COLS), blk(IDX_HEADS), blk(KV_COLS), blk(KV_COLS), blk(IDX_DIM),
                  pl.BlockSpec((1, HEAD_DIM), lambda b, pt: (0, 0)),
                  pl.BlockSpec(memory_space=pl.ANY), pl.BlockSpec(memory_space=pl.ANY),
                  pl.BlockSpec(memory_space=pl.ANY)],
        out_specs=blk(Q_COLS),
        scratch_shapes=[pltpu.VMEM((2, KV_COLS, past), F32), pltpu.VMEM((2, KV_COLS, past), F32),
                        pltpu.VMEM((2, IDX_DIM, past), F32), pltpu.SemaphoreType.DMA((3, 2)),
                        pltpu.VMEM((tp, 1), I32)])
    return pl.pallas_call(
        functools.partial(_attn_sample_kernel, n_pages=n_pages, t_new=t_new, ksel=ksel),
        grid_spec=grid_spec, out_shape=jax.ShapeDtypeStruct((nb, tp, Q_COLS), BF16),
        compiler_params=_cparams("arbitrary"), name="attn_sample",
    )(page_table, q, qi, wi, k_new, v_new, ki_new, g_q.reshape(1, HEAD_DIM), ck, cv, cki)


def _attn_out_ffn_kernel(x_ref, o_ref, wo_ref, ng_ref, wg_ref, wu_ref, wd_ref, out_ref):
    h = x_ref[...] + _dot(o_ref[...], wo_ref[...])
    xn = _rms(h, ng_ref[...]).astype(BF16)
    gate = _dot(xn, wg_ref[...])
    up = _dot(xn, wu_ref[...])
    act = (gate * jax.nn.sigmoid(gate) * up).astype(BF16)
    out_ref[...] = h + _dot(act, wd_ref[...])


def _resident(shape):
    n = len(shape)
    return pl.BlockSpec(shape, lambda *_: (0,) * n, pipeline_mode=pl.Buffered(1))


def _attn_out_ffn(x2d, o, w_o, norm_g, w_gate, w_up, w_down):
    m = x2d.shape[0]
    tm = min(ROW_TILE, m)
    row = lambda cols: pl.BlockSpec((tm, cols), lambda i: (i, 0))
    return pl.pallas_call(
        _attn_out_ffn_kernel, grid=(m // tm,),
        in_specs=[row(D_MODEL), row(Q_COLS), _resident(w_o.shape), _full((1, D_MODEL)),
                  _resident(w_gate.shape), _resident(w_up.shape), _resident(w_down.shape)],
        out_specs=row(D_MODEL), out_shape=jax.ShapeDtypeStruct((m, D_MODEL), F32),
        compiler_params=_cparams("parallel"), name="attn_out_ffn",
    )(x2d, o, w_o, norm_g.reshape(1, D_MODEL), w_gate, w_up, w_down)


def _s5_params_kernel(are_ref, aim_ref, ldt_ref, bre_ref, bim_ref, abr_ref, abi_ref, bbr_ref, bbi_ref):
    dt = jnp.exp(ldt_ref[...])
    lr = are_ref[...] * dt
    li = aim_ref[...] * dt
    mag = jnp.exp(lr)
    abr = mag * jnp.cos(li)
    abi = mag * jnp.sin(li)
    den = lr * lr + li * li
    fr = ((abr - 1.0) * lr + abi * li) / den * dt
    fi = (abi * lr - (abr - 1.0) * li) / den * dt
    abr_ref[...] = abr
    abi_ref[...] = abi
    bre = bre_ref[...]
    bim = bim_ref[...]
    bbr_ref[...] = fr * bre - fi * bim
    bbi_ref[...] = fr * bim + fi * bre


def _s5_params(a_re, a_im, log_dt, b_re, b_im):
    g, p, c = b_re.shape
    bt = lambda b: jnp.transpose(b, (0, 2, 1))
    shp3 = jax.ShapeDtypeStruct((g, 1, p), F32)
    shpb = jax.ShapeDtypeStruct((g, c, p), F32)
    return pl.pallas_call(
        _s5_params_kernel, out_shape=[shp3, shp3, shpb, shpb], name="s5_params",
    )(a_re.reshape(g, 1, p), a_im.reshape(g, 1, p), log_dt.reshape(g, 1, 1), bt(b_re), bt(b_im))


def _block_diag_tiles(w, groups_per_tile):
    g, a, b = w.shape
    eye = jnp.eye(groups_per_tile, dtype=w.dtype)
    w = w.reshape(g // groups_per_tile, groups_per_tile, a, b)
    out = w[:, :, :, None, :] * eye[None, :, None, :, None]
    return out.reshape(g // groups_per_tile, groups_per_tile * a, groups_per_tile * b)


def _s5_core_kernel(x_ref, ng_ref, win_ref, perm_ref, permt_ref, bbr_ref, bbi_ref, cr_ref, ci_ref,
                    abr_ref, abi_ref, d_ref, wglu_ref, s0r_ref, s0i_ref, h_ref, sr_ref, si_ref, *x_refs,
                    nb, tt):
    i = pl.program_id(0)
    rows = nb * tt
    gpt = MXU_DIM // SSM_GROUP
    n_kt = D_MODEL // MXU_DIM
    st_cols = gpt * SSM_STATE
    xr_refs, xi_refs = x_refs[:n_kt], x_refs[n_kt:]

    @pl.when(i == 0)
    def _():
        sr_ref[...] = s0r_ref[...]
        si_ref[...] = s0i_ref[...]

    x = x_ref[...].reshape(rows, D_MODEL)
    xn = _rms(x, ng_ref[...]).astype(BF16)
    xn = _dot(perm_ref[...], xn).astype(BF16)
    u = _dot(xn, win_ref[...])
    ub = u.astype(BF16)
    sr0, si0 = sr_ref[...], si_ref[...]

    cw = min(S5_SCAN_ELEMS // nb, st_cols)
    ys, srs, sis = [], [], []
    for kt in range(n_kt):
        xr_ref, xi_ref = xr_refs[kt], xi_refs[kt]
        uk = ub[:, kt * MXU_DIM:(kt + 1) * MXU_DIM]
        xr_ref[...] = _dot(uk, bbr_ref[kt])
        xi_ref[...] = _dot(uk, bbi_ref[kt])
        for c0 in range(0, st_cols, cw):
            g0 = kt * st_cols + c0
            ar = jnp.broadcast_to(abr_ref[:, g0:g0 + cw], (nb, cw))
            ai = jnp.broadcast_to(abi_ref[:, g0:g0 + cw], (nb, cw))
            hr, hi = sr0[:, g0:g0 + cw], si0[:, g0:g0 + cw]
            for t in range(tt):
                r = slice(t * nb, (t + 1) * nb)
                hr, hi = (ar * hr - ai * hi + xr_ref[r, c0:c0 + cw],
                          ar * hi + ai * hr + xi_ref[r, c0:c0 + cw])
                xr_ref[r, c0:c0 + cw] = hr
                xi_ref[r, c0:c0 + cw] = hi
            srs.append(hr)
            sis.append(hi)
        ys.append(_dot(xr_ref[...].astype(BF16), cr_ref[kt]) - _dot(xi_ref[...].astype(BF16), ci_ref[kt]))
    sr_ref[...] = jnp.concatenate(srs, axis=-1)
    si_ref[...] = jnp.concatenate(sis, axis=-1)
    y = jnp.concatenate(ys, axis=-1) + d_ref[...] * u
    z = jax.nn.gelu(y).astype(BF16)
    z = _dot(permt_ref[...], z).astype(BF16)
    gl = _dot(z, wglu_ref[...])
    h_ref[...] = (x + gl[:, :D_MODEL] * jax.nn.sigmoid(gl[:, D_MODEL:])).reshape(h_ref.shape)


def _s5_core(x, s0_re, s0_im, norm_g, w_in, abr, abi, bbr, bbi, c_re, c_im, d_skip, w_glu, *, tt):
    nb, t, _ = x.shape
    rows = nb * tt
    gpt = MXU_DIM // SSM_GROUP
    r = np.arange(rows)
    perm_np = np.zeros((rows, rows), np.float32)
    perm_np[r, (r % nb) * tt + r // nb] = 1.0
    perm = jnp.asarray(perm_np, BF16)
    permt = jnp.asarray(perm_np.T, BF16)
    bbr_t = _block_diag_tiles(bbr, gpt).astype(BF16)
    bbi_t = _block_diag_tiles(bbi, gpt).astype(BF16)
    ct = lambda c: _block_diag_tiles(jnp.transpose(c, (0, 2, 1)), gpt).astype(BF16)
    if tt % 8 == 0:
        x_in, x_spec = x, pl.BlockSpec((nb, tt, D_MODEL), lambda i: (0, i, 0))
    else:
        assert tt == t
        x_in, x_spec = x.reshape(rows, D_MODEL), pl.BlockSpec((rows, D_MODEL), lambda i: (0, 0))
    st = jax.ShapeDtypeStruct((nb, SSM_COLS), F32)
    cshape = (D_MODEL // MXU_DIM, gpt * SSM_STATE, MXU_DIM)
    h, sr, si = pl.pallas_call(
        functools.partial(_s5_core_kernel, nb=nb, tt=tt), grid=(t // tt,),
        in_specs=[x_spec, _full((1, D_MODEL)), _resident(w_in.shape), _full(perm.shape), _full(permt.shape),
                  _resident(bbr_t.shape), _resident(bbi_t.shape), _resident(cshape), _resident(cshape),
                  _full((1, SSM_COLS)), _full((1, SSM_COLS)), _full((1, D_MODEL)), _resident(w_glu.shape),
                  _full((nb, SSM_COLS)), _full((nb, SSM_COLS))],
        out_specs=[x_spec, _full((nb, SSM_COLS)), _full((nb, SSM_COLS))],
        out_shape=[jax.ShapeDtypeStruct(x_in.shape, F32), st, st],
        scratch_shapes=[pltpu.VMEM((rows, gpt * SSM_STATE), F32)] * (2 * (D_MODEL // MXU_DIM)),
        compiler_params=_cparams("arbitrary"), name="s5_layer",
    )(x_in, norm_g.reshape(1, D_MODEL), w_in, perm, permt, bbr_t, bbi_t, ct(c_re), ct(c_im),
      abr.reshape(1, SSM_COLS), abi.reshape(1, SSM_COLS), d_skip.reshape(1, D_MODEL), w_glu, s0_re, s0_im)
    return h.reshape(nb * t, D_MODEL), sr, si


def _moe_router_kernel(h_ref, ng_ref, wr_ref, br_ref, xn_ref, idx_ref, gate_ref, rank_ref, cnt_ref, carry_ref):
    i = pl.program_id(0)
    tm = h_ref.shape[0]

    @pl.when(i == 0)
    def _():
        carry_ref[...] = jnp.zeros_like(carry_ref)

    xn = _rms(h_ref[...], ng_ref[...])
    xn_ref[...] = xn
    logits = lax.dot_general(wr_ref[...], xn, (((1,), (1,)), ((), ())), preferred_element_type=F32,
                             precision=lax.Precision.HIGHEST) + br_ref[...]
    eid = lax.broadcasted_iota(I32, (N_EXPERTS, tm), 0).astype(F32)
    m1 = jnp.max(logits, axis=0, keepdims=True)
    i1 = jnp.min(jnp.where(logits == m1, eid, float(N_EXPERTS)), axis=0, keepdims=True)
    rest = jnp.where(eid == i1, -jnp.inf, logits)
    m2 = jnp.max(rest, axis=0, keepdims=True)
    i2 = jnp.min(jnp.where(rest == m2, eid, float(N_EXPERTS)), axis=0, keepdims=True)
    e2 = jnp.exp(m2 - m1)
    den = 1.0 + e2
    idx_ref[...] = jnp.concatenate([i1, i2], axis=0).astype(I32)
    gate_ref[...] = jnp.concatenate([1.0 / den, e2 / den], axis=0)

    onehot = jnp.where((eid == i1) | (eid == i2), 1.0, 0.0)
    src = lax.broadcasted_iota(I32, (tm, tm), 0)
    dst = lax.broadcasted_iota(I32, (tm, tm), 1)
    tri = jnp.where(src <= dst, 1.0, 0.0).astype(BF16)
    cum = _dot(onehot.astype(BF16), tri) + carry_ref[:, 0:1]
    before = cum - 1.0
    r1 = jnp.sum(jnp.where(eid == i1, before, 0.0), axis=0, keepdims=True)
    r2 = jnp.sum(jnp.where(eid == i2, before, 0.0), axis=0, keepdims=True)
    rank_ref[...] = jnp.concatenate([r1, r2], axis=0).astype(I32)
    total = cum[:, tm - 1:tm]
    carry_ref[...] = jnp.broadcast_to(total, carry_ref.shape)
    cnt_ref[...] = jnp.broadcast_to(total, cnt_ref.shape).astype(I32)


def _moe_router(h2d, norm_g, w_router, b_router):
    m = h2d.shape[0]
    tm = min(ROW_TILE, m)
    row = pl.BlockSpec((tm, D_MODEL), lambda i: (i, 0))
    col2 = pl.BlockSpec((2, tm), lambda i: (0, i))
    return pl.pallas_call(
        _moe_router_kernel, grid=(m // tm,),
        in_specs=[row, _full((1, D_MODEL)), _full((N_EXPERTS, D_MODEL)), _full((N_EXPERTS, 1))],
        out_specs=[row, col2, col2, col2, _full((N_EXPERTS, LANES))],
        out_shape=[jax.ShapeDtypeStruct((m, D_MODEL), F32), jax.ShapeDtypeStruct((2, m), I32),
                   jax.ShapeDtypeStruct((2, m), F32), jax.ShapeDtypeStruct((2, m), I32),
                   jax.ShapeDtypeStruct((N_EXPERTS, LANES), I32)],
        scratch_shapes=[pltpu.VMEM((N_EXPERTS, LANES), F32)],
        compiler_params=_cparams("arbitrary"), name="moe_router",
    )(h2d, norm_g.reshape(1, D_MODEL), w_router.T, b_router.reshape(N_EXPERTS, 1))


def _moe_scatter_kernel(tend_ref, pos_ref, xn_ref, xs_hbm, zbuf, sem, zsem, *, n_tiles):
    i = pl.program_id(0)
    ch = pos_ref.shape[1]
    tg = zbuf.shape[0]

    @pl.when(i == 0)
    def _():
        zbuf[...] = jnp.zeros_like(zbuf)
        used = tend_ref[N_EXPERTS - 1]

        def clears():
            for e in range(N_EXPERTS):
                first = tend_ref[e - 1] if e else 0
                yield tend_ref[e] > first, tend_ref[e] - 1
            for r in range(N_EXPERTS):
                yield used + r < n_tiles, used + r

        for go, tile in clears():
            @pl.when(go)
            def _():
                pltpu.make_async_copy(zbuf, xs_hbm.at[pl.ds(tile * tg, tg)], zsem).start()
        for go, tile in clears():
            @pl.when(go)
            def _():
                pltpu.make_async_copy(zbuf, xs_hbm.at[pl.ds(tile * tg, tg)], zsem).wait()

    def copy(t, k):
        return pltpu.make_async_copy(xn_ref.at[pl.ds(t, 1)], xs_hbm.at[pl.ds(pos_ref[k, t], 1)], sem)

    def issue(t, c):
        copy(t, 0).start()
        copy(t, 1).start()
        return c

    def drain(t, c):
        copy(t, 0).wait()
        copy(t, 1).wait()
        return c

    lax.fori_loop(0, ch, issue, 0, unroll=8)
    lax.fori_loop(0, ch, drain, 0, unroll=8)


def _moe_scatter(xn, pos, tile_end, n_tiles):
    m = xn.shape[0]
    ch = min(MOE_DMA_CHUNK, m)
    tg = MOE_ROW_TILE
    grid_spec = pltpu.PrefetchScalarGridSpec(
        num_scalar_prefetch=1, grid=(m // ch,),
        in_specs=[pl.BlockSpec((2, ch), lambda i, te: (0, i), memory_space=pltpu.SMEM),
                  pl.BlockSpec((ch, D_MODEL), lambda i, te: (i, 0))],
        out_specs=pl.BlockSpec(memory_space=pl.ANY),
        scratch_shapes=[pltpu.VMEM((tg, D_MODEL), F32), pltpu.SemaphoreType.DMA(()),
                        pltpu.SemaphoreType.DMA(())])
    return pl.pallas_call(
        functools.partial(_moe_scatter_kernel, n_tiles=n_tiles), grid_spec=grid_spec,
        out_shape=jax.ShapeDtypeStruct((n_tiles * tg, D_MODEL), F32),
        compiler_params=_cparams("arbitrary"), name="moe_scatter",
    )(tile_end, pos, xn)


def _moe_expert_kernel(te_ref, nt_ref, xs_ref, wg_ref, wu_ref, wd_ref, ys_ref):
    i = pl.program_id(0)

    @pl.when(i < nt_ref[0])
    def _():
        x = xs_ref[...].astype(BF16)
        gate = _dot(x, wg_ref[0])
        up = _dot(x, wu_ref[0])
        act = (gate * jax.nn.sigmoid(gate) * up).astype(BF16)
        ys_ref[...] = _dot(act, wd_ref[0])

    @pl.when(i >= nt_ref[0])
    def _():
        ys_ref[...] = jnp.zeros_like(ys_ref)


def _moe_experts(xs, tile_expert, n_tiles_used, w_gate, w_up, w_down):
    n_rows = xs.shape[0]
    tg = MOE_ROW_TILE
    d_ff = w_gate.shape[2]
    row = pl.BlockSpec((tg, D_MODEL), lambda i, te, nt: (i, 0))
    grid_spec = pltpu.PrefetchScalarGridSpec(
        num_scalar_prefetch=2, grid=(n_rows // tg,),
        in_specs=[row,
                  pl.BlockSpec((1, D_MODEL, d_ff), lambda i, te, nt: (te[i], 0, 0)),
                  pl.BlockSpec((1, D_MODEL, d_ff), lambda i, te, nt: (te[i], 0, 0)),
                  pl.BlockSpec((1, d_ff, D_MODEL), lambda i, te, nt: (te[i], 0, 0))],
        out_specs=row)
    return pl.pallas_call(
        _moe_expert_kernel, grid_spec=grid_spec, out_shape=jax.ShapeDtypeStruct((n_rows, D_MODEL), F32),
        compiler_params=_cparams("arbitrary"), name="moe_experts",
    )(tile_expert, n_tiles_used, xs, w_gate, w_up, w_down)


def _moe_combine_kernel(pos_ref, pos_next_ref, h_ref, gate_ref, ys_hbm, out_ref, buf, sems):
    i = pl.program_id(0)
    n = pl.num_programs(0)
    tm = h_ref.shape[0]
    slot = i % 2

    def copy(p_ref, s, t, k):
        return pltpu.make_async_copy(ys_hbm.at[pl.ds(p_ref[k, t], 1)], buf.at[s, k, pl.ds(t, 1)], sems.at[s])

    def issue(p_ref, s):
        def body(t, c):
            copy(p_ref, s, t, 0).start()
            copy(p_ref, s, t, 1).start()
            return c
        lax.fori_loop(0, tm, body, 0, unroll=8)

    @pl.when(i == 0)
    def _():
        issue(pos_ref, 0)

    @pl.when(i + 1 < n)
    def _():
        issue(pos_next_ref, 1 - slot)

    def drain(t, c):
        copy(pos_ref, slot, t, 0).wait()
        copy(pos_ref, slot, t, 1).wait()
        return c

    lax.fori_loop(0, tm, drain, 0, unroll=8)
    gate = gate_ref[...]
    out_ref[...] = h_ref[...] + gate[:, 0:1] * buf[slot, 0] + gate[:, 1:2] * buf[slot, 1]


def _moe_combine(h2d, gates_t, pos, ys):
    m = h2d.shape[0]
    tm = min(MOE_DMA_CHUNK, m)
    n = m // tm
    row = pl.BlockSpec((tm, D_MODEL), lambda i: (i, 0))
    return pl.pallas_call(
        _moe_combine_kernel, grid=(n,),
        in_specs=[pl.BlockSpec((2, tm), lambda i: (0, i), memory_space=pltpu.SMEM),
                  pl.BlockSpec((2, tm), lambda i: (0, jnp.minimum(i + 1, n - 1)), memory_space=pltpu.SMEM),
                  row, pl.BlockSpec((tm, 2), lambda i: (i, 0)), pl.BlockSpec(memory_space=pl.ANY)],
        out_specs=row, out_shape=jax.ShapeDtypeStruct((m, D_MODEL), F32),
        scratch_shapes=[pltpu.VMEM((2, 2, tm, D_MODEL), F32), pltpu.SemaphoreType.DMA((2,))],
        compiler_params=_cparams("arbitrary"), name="moe_combine",
    )(pos, pos, h2d, gates_t, ys)


def _moe_layer(h2d, norm_g, w_router, b_router, w_gate, w_up, w_down):
    m = h2d.shape[0]
    tg = MOE_ROW_TILE
    xn, idx, gate, rank, cnt = _moe_router(h2d, norm_g, w_router, b_router)
    counts = cnt[:, 0]
    tiles = (counts + tg - 1) // tg
    tile_end = jnp.cumsum(tiles).astype(I32)
    base = (tile_end - tiles) * tg
    pos = rank
    for e in range(N_EXPERTS):
        pos = pos + jnp.where(idx == e, base[e], 0)
    n_tiles = (2 * m) // tg + N_EXPERTS
    tile_expert = jnp.sum((jnp.arange(n_tiles, dtype=I32)[:, None] >= tile_end[None, :]).astype(I32), axis=1)
    tile_expert = jnp.minimum(tile_expert, N_EXPERTS - 1)
    xs = _moe_scatter(xn, pos, tile_end, n_tiles)
    ys = _moe_experts(xs, tile_expert, tile_end[-1:], w_gate, w_up, w_down)
    return _moe_combine(h2d, gate.T, pos, ys)


def kernel(x_prompt, x_sample, cache_k, cache_v, cache_kidx, state_s5_re, state_s5_im, page_table,
           attn_norm_g, attn_w_in, attn_g_q, attn_g_k, attn_g_kidx, attn_w_o,
           dense_norm_g, dense_w_gate, dense_w_up, dense_w_down,
           s5_norm_g, s5_w_in, s5_a_re, s5_a_im, s5_log_dt, s5_b_re, s5_b_im, s5_c_re, s5_c_im,
           s5_d, s5_w_glu,
           moe_norm_g, moe_w_router, moe_b_router, moe_w_gate, moe_w_up, moe_w_down):
    nbp, seq, d = x_prompt.shape
    nbs, t_new, _ = x_sample.shape
    mp, ms = nbp * seq, nbs * t_new
    tp = 8
    xp2 = x_prompt.reshape(mp, d)
    xs2 = x_sample.reshape(ms, d)

    w_in_t = attn_w_in[0].T.astype(BF16)
    o1, o2, o3, o4, o5 = np.cumsum([Q_COLS, KV_COLS, KV_COLS, QI_COLS, IDX_DIM]).tolist()
    w_kw = jnp.pad(w_in_t[o4:], ((0, LANES - (w_in_t.shape[0] - o4)), (0, 0)))
    w_parts = (w_in_t[:o1], w_in_t[o1:o3], w_in_t[o3:o4], w_kw)
    ap = (attn_norm_g[0], w_parts, attn_g_k[0], attn_g_kidx[0])

    q_p, qi_p, wi_p, kt_p, vt_p, kit_p, ktb, vtb, kitb = _attn_proj(xp2, *ap, seq_len=seq)
    o_p = _attn_prompt(q_p, qi_p, wi_p, ktb, vtb, kitb, attn_g_q[0])

    q_s, qi_s, wi_s, k_s, v_s, ki_s = _attn_proj(xs2, *ap)
    pad_t = lambda a: jnp.pad(a.reshape(nbs, t_new, a.shape[-1]), ((0, 0), (0, tp - t_new), (0, 0)))
    n_pool = cache_k.shape[1]
    ck = jnp.transpose(cache_k[0], (0, 2, 3, 1)).reshape(n_pool, KV_COLS, PAGE_SIZE)
    cv = jnp.transpose(cache_v[0], (0, 2, 3, 1)).reshape(n_pool, KV_COLS, PAGE_SIZE)
    cki = jnp.transpose(cache_kidx[0], (0, 2, 1))
    o_s = _attn_sample(pad_t(q_s), pad_t(qi_s.astype(F32)), pad_t(wi_s), pad_t(k_s), pad_t(v_s), pad_t(ki_s),
                       attn_g_q[0], page_table, ck, cv, cki, t_new=t_new)
    o_s = o_s[:, :t_new].reshape(ms, Q_COLS)

    fp = (attn_w_o[0].astype(BF16), dense_norm_g[0], dense_w_gate[0].astype(BF16),
          dense_w_up[0].astype(BF16), dense_w_down[0].astype(BF16))
    hp = _attn_out_ffn(xp2, o_p, *fp)
    hs = _attn_out_ffn(xs2, o_s, *fp)

    abr, abi, bbr, bbi = _s5_params(s5_a_re[0], s5_a_im[0], s5_log_dt[0], s5_b_re[0], s5_b_im[0])
    sp = (s5_norm_g[0], s5_w_in[0].astype(BF16), abr, abi, bbr, bbi, s5_c_re[0], s5_c_im[0], s5_d[0],
          s5_w_glu[0].astype(BF16))
    zeros_p = jnp.zeros((nbp, SSM_COLS), F32)
    hp, srp, sip = _s5_core(hp.reshape(nbp, seq, d), zeros_p, zeros_p, *sp, tt=S5_TIME_TILE)
    hs, srs, sis = _s5_core(hs.reshape(nbs, t_new, d), state_s5_re[0].reshape(nbs, SSM_COLS),
                            state_s5_im[0].reshape(nbs, SSM_COLS), *sp, tt=t_new)

    mw = (moe_norm_g[0], moe_w_router[0], moe_b_router[0], moe_w_gate[0].astype(BF16),
          moe_w_up[0].astype(BF16), moe_w_down[0].astype(BF16))
    hp = _moe_layer(hp, *mw)
    hs = _moe_layer(hs, *mw)

    fm_kv = lambda a: jnp.transpose(a.reshape(nbp, N_KV_HEADS, HEAD_DIM, seq), (0, 3, 1, 2))[None]
    st = lambda a, n: a.reshape(1, n, SSM_GROUPS, SSM_STATE)
    return (hp.reshape(nbp, seq, d), hs.reshape(nbs, t_new, d),
            fm_kv(kt_p), fm_kv(vt_p), jnp.transpose(kit_p, (0, 2, 1))[None],
            st(srp, nbp), st(sip, nbp),
            k_s.reshape(1, nbs, t_new, N_KV_HEADS, HEAD_DIM), v_s.reshape(1, nbs, t_new, N_KV_HEADS, HEAD_DIM),
            ki_s.reshape(1, nbs, t_new, IDX_DIM),
            st(srs, nbs), st(sis, nbs))
```
